```python
import jax
import jax.numpy as jnp
from jax import lax
import numpy as np

D_MODEL = 1024
BATCH = 16
SEQ = 2048
DEPTH = 2

Q_BLOCK = 128
LN_EPS = 1e-5
GLA_HEADS = 4
GLA_DK = 48
GLA_DV = 96
GLA_GATE_RANK = 16
GLA_TAU = 16.0
GLA_CHUNK = 64
DSA_HEADS = 5
DSA_DH = 64
DSA_LATENT = 128
IDX_HEADS = 8
IDX_DIM = 64
DSA_TOPK_MAX = 256
SB_HEADS = 5
SB_DH = 64
GLA_WIDTH = GLA_HEADS * GLA_DV
DSA_WIDTH = DSA_HEADS * DSA_DH
SB_WIDTH = SB_HEADS * SB_DH
MIX_WIDTH = GLA_WIDTH + DSA_WIDTH + SB_WIDTH
MEM_TOKENS = 256
MEM_HEADS = 4
MEM_DH = D_MODEL // MEM_HEADS
D_FF = 4 * D_MODEL
ALPHA = (2.0 * DEPTH) ** 0.25
BETA = (8.0 * DEPTH) ** -0.25
ALIBI_SLOPES = tuple(2.0 ** (-8.0 * (i + 1) / DSA_HEADS) for i in range(DSA_HEADS))
COL_SIZES = (
    GLA_HEADS * GLA_DK,
    GLA_HEADS * GLA_DK,
    GLA_WIDTH,
    GLA_GATE_RANK,
    GLA_WIDTH,
    DSA_WIDTH,
    DSA_DH,
    DSA_LATENT,
    IDX_HEADS * IDX_DIM,
    IDX_DIM,
    IDX_HEADS,
    SB_WIDTH,
    SB_WIDTH,
    SB_WIDTH,
)
P_IN = sum(COL_SIZES)
SPLIT_POINTS = tuple(int(v) for v in np.cumsum(COL_SIZES)[:-1])

kernel_name = 'hybrid_gla_dsa_stickbreaking_deepnorm'


def layer_norm(x, g, b):
    xf = x.astype(jnp.float32)
    mu = jnp.mean(xf, axis=-1, keepdims=True)
    var = jnp.mean(jnp.square(xf - mu), axis=-1, keepdims=True)
    return ((xf - mu) * lax.rsqrt(var + LN_EPS) * g + b).astype(x.dtype)


def gla_mixer(q, k, v, gate_lr, g, gate_w2, gate_b, norm_g):
    B, L = q.shape[0], q.shape[1]
    H, dk, dv, C = GLA_HEADS, GLA_DK, GLA_DV, GLA_CHUNK
    N = L // C
    f32 = jnp.float32
    log_a = jax.nn.log_sigmoid((gate_lr @ gate_w2 + gate_b).astype(f32)) / GLA_TAU
    log_a = log_a.reshape(B, N, C, H, dk)
    qf = q.astype(f32).reshape(B, N, C, H, dk) * (dk ** -0.5)
    kf = k.astype(f32).reshape(B, N, C, H, dk)
    vf = v.astype(f32).reshape(B, N, C, H, dv)
    bcum = jnp.cumsum(log_a, axis=2)
    b_last = bcum[:, :, -1]
    q_t = qf * jnp.exp(bcum)
    k_t = kf * jnp.exp(-bcum)
    k_end = kf * jnp.exp(b_last[:, :, None] - bcum)
    causal = jnp.tril(jnp.ones((C, C), dtype=bool))
    scores = jnp.einsum('bnchd,bnshd->bnhcs', q_t, k_t)
    scores = jnp.where(causal, scores, 0.0)
    o_intra = jnp.einsum('bnhcs,bnshv->bnchv', scores, vf)
    dS = jnp.einsum('bnchd,bnchv->nbhdv', k_end, vf)
    decay = jnp.exp(jnp.moveaxis(b_last, 1, 0))

    def step(S, inp):
        dS_n, dec_n = inp
        return dec_n[..., None] * S + dS_n, S

    S0 = jnp.zeros((B, H, dk, dv), f32)
    _, S_prev = lax.scan(step, S0, (dS, decay))
    o_inter = jnp.einsum('bnchd,nbhdv->bnchv', q_t, S_prev)
    o = (o_intra + o_inter).reshape(B, L, H, dv)
    o = o * lax.rsqrt(jnp.mean(jnp.square(o), axis=-1, keepdims=True) + LN_EPS) * norm_g
    o = o * jax.nn.silu(g.astype(f32))
    return o.reshape(B, L, H * dv).astype(q.dtype)


def dsa_mixer(q, k_sh, v_lat, iq, ik, iw, w_uv):
    B, L = q.shape[0], q.shape[1]
    f32 = jnp.float32
    topk = min(DSA_TOPK_MAX, L // 4)
    nb = L // Q_BLOCK
    key_pos = jnp.arange(L)
    kv = jnp.concatenate([k_sh, v_lat], axis=-1)
    slopes = jnp.asarray(ALIBI_SLOPES, f32)
    idx_scale = (IDX_HEADS ** -0.5) * (IDX_DIM ** -0.5)

    def block(i):
        t0 = i * Q_BLOCK
        qb = lax.dynamic_slice_in_dim(q, t0, Q_BLOCK, axis=1)
        iqb = lax.dynamic_slice_in_dim(iq, t0, Q_BLOCK, axis=1)
        iwb = lax.dynamic_slice_in_dim(iw, t0, Q_BLOCK, axis=1)
        qpos = t0 + jnp.arange(Q_BLOCK)
        causal = key_pos[None, :] <= qpos[:, None]
        rel = jax.nn.relu(jnp.einsum('bthd,bsd->bths', iqb, ik).astype(f32))
        isc = jnp.einsum('bths,bth->bts', rel, iwb.astype(f32)) * idx_scale
        isc = jnp.where(causal[None], isc, -jnp.inf)
        _, idx = lax.top_k(isc, topk)
        valid = idx <= qpos[None, :, None]
        sel = jax.vmap(lambda a, j: a[j])(kv, idx)
        k_sel = sel[..., :DSA_DH]
        v_sel = sel[..., DSA_DH:]
        dist = (qpos[None, :, None] - idx).astype(f32)
        s = jnp.einsum('bthd,btkd->bhtk', qb, k_sel).astype(f32) * (DSA_DH ** -0.5)
        s = s - slopes[None, :, None, None] * dist[:, None]
        s = jnp.where(valid[:, None], s, -jnp.inf)
        p = jax.nn.softmax(s, axis=-1).astype(v_sel.dtype)
        o_lat = jnp.einsum('bhtk,btkc->bthc', p, v_sel)
        o = jnp.einsum('bthc,hcd->bthd', o_lat, w_uv)
        return o.reshape(B, Q_BLOCK, DSA_WIDTH)

    out = lax.map(block, jnp.arange(nb))
    return jnp.moveaxis(out, 0, 1).reshape(B, L, DSA_WIDTH)


def sb_mixer(q, k, v):
    B, L = q.shape[0], q.shape[1]
    f32 = jnp.float32
    nb = L // Q_BLOCK
    key_pos = jnp.arange(L)

    def block(i):
        t0 = i * Q_BLOCK
        qb = lax.dynamic_slice_in_dim(q, t0, Q_BLOCK, axis=1)
        qpos = t0 + jnp.arange(Q_BLOCK)
        strict = key_pos[None, :] < qpos[:, None]
        z = jnp.einsum('bthd,bshd->bhts', qb, k).astype(f32) * (SB_DH ** -0.5)
        log_1mb = jnp.where(strict, jax.nn.log_sigmoid(-z), 0.0)
        rev = lax.cumsum(log_1mb, axis=3, reverse=True)
        suffix = jnp.concatenate([rev[..., 1:], jnp.zeros_like(rev[..., :1])], axis=-1)
        w = jnp.where(strict, jnp.exp(jax.nn.log_sigmoid(z) + suffix), 0.0)
        o = jnp.einsum('bhts,bshd->bthd', w.astype(v.dtype), v)
        return o.reshape(B, Q_BLOCK, SB_WIDTH)

    out = lax.map(block, jnp.arange(nb))
    return jnp.moveaxis(out, 0, 1).reshape(B, L, SB_WIDTH)


def hybrid_mixer(h, w_in, gate_w2, gate_b, gla_norm_g, w_uv, w_out):
    B, L = h.shape[0], h.shape[1]
    proj = h @ w_in
    (gq, gk, gv, glr, gg, dq, dk, dv, iq, ik, iw, sq, sk, sv) = jnp.split(proj, SPLIT_POINTS, axis=-1)
    o_gla = gla_mixer(gq.reshape(B, L, GLA_HEADS, GLA_DK), gk.reshape(B, L, GLA_HEADS, GLA_DK),
                      gv.reshape(B, L, GLA_HEADS, GLA_DV), glr, gg.reshape(B, L, GLA_HEADS, GLA_DV),
                      gate_w2, gate_b, gla_norm_g)
    o_dsa = dsa_mixer(dq.reshape(B, L, DSA_HEADS, DSA_DH), dk, dv,
                      iq.reshape(B, L, IDX_HEADS, IDX_DIM), ik, iw, w_uv)
    o_sb = sb_mixer(sq.reshape(B, L, SB_HEADS, SB_DH), sk.reshape(B, L, SB_HEADS, SB_DH),
                    sv.reshape(B, L, SB_HEADS, SB_DH))
    o = jnp.concatenate([o_gla, o_dsa, o_sb], axis=-1)
    return o @ w_out


def mem_attn(h, mem, w_q, w_kv, w_o):
    B, L = h.shape[0], h.shape[1]
    M = mem.shape[1]
    q = (h @ w_q).reshape(B, L, MEM_HEADS, MEM_DH)
    kv = mem @ w_kv
    k = kv[..., :D_MODEL].reshape(B, M, MEM_HEADS, MEM_DH)
    v = kv[..., D_MODEL:].reshape(B, M, MEM_HEADS, MEM_DH)
    s = jnp.einsum('blhd,bmhd->bhlm', q, k).astype(jnp.float32) * (MEM_DH ** -0.5)
    p = jax.nn.softmax(s, axis=-1).astype(v.dtype)
    o = jnp.einsum('bhlm,bmhd->blhd', p, v).reshape(B, L, D_MODEL)
    return o @ w_o


def sq_relu_mlp(h, w_up, b_up, w_down, b_down):
    u = jax.nn.relu(h @ w_up + b_up)
    return (u * u) @ w_down + b_down


def setup_inputs(seed: int = 0) -> dict:
    key = jax.random.key(seed)
    ks = jax.random.split(key, 26)
    f32 = jnp.float32

    def nrm(k, shape, scale):
        return jax.random.normal(k, shape, f32) * scale

    def gain(k, shape):
        return 1.0 + 0.02 * jax.random.normal(k, shape, f32)

    return {
        'x': nrm(ks[0], (BATCH, SEQ, D_MODEL), 1.0),
        'mem': nrm(ks[1], (BATCH, MEM_TOKENS, D_MODEL), 1.0),
        'ln_in_g': gain(ks[2], (D_MODEL,)),
        'ln_in_b': nrm(ks[3], (D_MODEL,), 0.02),
        'w_in': nrm(ks[4], (DEPTH, D_MODEL, P_IN), D_MODEL ** -0.5),
        'gla_gate_w2': nrm(ks[5], (DEPTH, GLA_GATE_RANK, GLA_HEADS * GLA_DK), GLA_GATE_RANK ** -0.5),
        'gla_gate_b': nrm(ks[6], (DEPTH, GLA_HEADS * GLA_DK), 0.1),
        'gla_norm_g': gain(ks[7], (DEPTH, GLA_DV)),
        'dsa_w_uv': nrm(ks[8], (DEPTH, DSA_HEADS, DSA_LATENT, DSA_DH), DSA_LATENT ** -0.5),
        'w_out': nrm(ks[9], (DEPTH, MIX_WIDTH, D_MODEL), BETA * MIX_WIDTH ** -0.5),
        'ln_mix_g': gain(ks[10], (DEPTH, D_MODEL)),
        'ln_mix_b': nrm(ks[11], (DEPTH, D_MODEL), 0.02),
        'w_mem_q': nrm(ks[12], (DEPTH, D_MODEL, D_MODEL), D_MODEL ** -0.5),
        'w_mem_kv': nrm(ks[13], (DEPTH, D_MODEL, 2 * D_MODEL), D_MODEL ** -0.5),
        'w_mem_o': nrm(ks[14], (DEPTH, D_MODEL, D_MODEL), BETA * D_MODEL ** -0.5),
        'ln_mem_g': gain(ks[15], (DEPTH, D_MODEL)),
        'ln_mem_b': nrm(ks[16], (DEPTH, D_MODEL), 0.02),
        'w_up': nrm(ks[17], (DEPTH, D_MODEL, D_FF), D_MODEL ** -0.5),
        'b_up': nrm(ks[18], (DEPTH, D_FF), 0.02),
        'w_down': nrm(ks[19], (DEPTH, D_FF, D_MODEL), BETA * D_FF ** -0.5),
        'b_down': nrm(ks[20], (DEPTH, D_MODEL), 0.02),
        'ln_ffn_g': gain(ks[21], (DEPTH, D_MODEL)),
        'ln_ffn_b': nrm(ks[22], (DEPTH, D_MODEL), 0.02),
    }


def reference(x, mem, ln_in_g, ln_in_b, w_in, gla_gate_w2, gla_gate_b, gla_norm_g, dsa_w_uv,
              w_out, ln_mix_g, ln_mix_b, w_mem_q, w_mem_kv, w_mem_o, ln_mem_g, ln_mem_b,
              w_up, b_up, w_down, b_down, ln_ffn_g, ln_ffn_b):
    h = layer_norm(x, ln_in_g, ln_in_b)
    for l in range(DEPTH):
        f = hybrid_mixer(h, w_in[l], gla_gate_w2[l], gla_gate_b[l], gla_norm_g[l], dsa_w_uv[l], w_out[l])
        h = layer_norm(ALPHA * h + f, ln_mix_g[l], ln_mix_b[l])
        f = mem_attn(h, mem, w_mem_q[l], w_mem_kv[l], w_mem_o[l])
        h = layer_norm(ALPHA * h + f, ln_mem_g[l], ln_mem_b[l])
        f = sq_relu_mlp(h, w_up[l], b_up[l], w_down[l], b_down[l])
        h = layer_norm(ALPHA * h + f, ln_ffn_g[l], ln_ffn_b[l])
    return h
```

```python
import functools

import jax
import jax.numpy as jnp
from jax import lax
from jax.experimental import pallas as pl
from jax.experimental.pallas import tpu as pltpu

F32 = jnp.float32
BF16 = jnp.bfloat16

DEPTH = 2
LN_EPS = 1e-5
GLA_HEADS, GLA_DK, GLA_DV, GLA_RANK, GLA_TAU, GLA_CHUNK = 4, 48, 96, 16, 16.0, 64
GLA_DKP, GLA_DVP = 64, 128
DSA_HEADS, DSA_DH, DSA_LATENT = 5, 64, 128
IDX_HEADS, IDX_DIM, DSA_TOPK_MAX = 8, 64, 256
SB_HEADS, SB_DH = 5, 64
MEM_HEADS = 4
ALPHA = (2.0 * DEPTH) ** 0.25
ALIBI_SLOPES = tuple(2.0 ** (-8.0 * (i + 1) / DSA_HEADS) for i in range(DSA_HEADS))
NEG_INF = float("-inf")
INT_MIN = -(2 ** 31)

LANE = 128
VMEM_LIMIT = 56 * 1024 * 1024


def _dot(a, b):
    return jnp.dot(a, b, preferred_element_type=F32)


def _dot_nt(a, b):
    return lax.dot_general(a, b, (((1,), (1,)), ((), ())), preferred_element_type=F32)


def _ln(x, g, b):
    mu = jnp.mean(x, axis=-1, keepdims=True)
    xc = x - mu
    var = jnp.mean(xc * xc, axis=-1, keepdims=True)
    return xc * lax.rsqrt(var + LN_EPS) * g + b


def _softplus(x):
    return jnp.maximum(x, 0.0) + jnp.log1p(jnp.exp(-jnp.abs(x)))


def _split_dot(x, m, terms):
    out = None
    r = x
    for t in range(terms):
        xb = r.astype(BF16)
        d = _dot(xb, m)
        out = d if out is None else out + d
        if t + 1 < terms:
            r = r - xb.astype(F32)
    return out


def _split_dot_left(m, x, terms):
    out = None
    r = x
    for t in range(terms):
        xb = r.astype(BF16)
        d = _dot(m, xb)
        out = d if out is None else out + d
        if t + 1 < terms:
            r = r - xb.astype(F32)
    return out


def _params(sem):
    return pltpu.CompilerParams(dimension_semantics=sem, vmem_limit_bytes=VMEM_LIMIT)


def _inproj_body(*refs, apply_ln, n_out):
    if apply_ln:
        x_ref, g_ref, b_ref = refs[:3]
        rest = refs[3:]
    else:
        x_ref = refs[0]
        rest = refs[1:]
    w_refs = rest[:n_out]
    o_refs = rest[n_out:2 * n_out]
    x = x_ref[...]
    if apply_ln:
        x = _ln(x, g_ref[...], b_ref[...])
        rest[2 * n_out][...] = x
    xb = x.astype(BF16)
    for w_ref, o_ref in zip(w_refs, o_refs):
        n = w_ref.shape[1]
        for n0 in range(0, n, 512):
            n1 = min(n0 + 512, n)
            o_ref[:, n0:n1] = _dot(xb, w_ref[:, n0:n1]).astype(o_ref.dtype)


def _inproj(x, ws, out_dtypes, ln=None, tm=512):
    m, d = x.shape
    n_out = len(ws)
    apply_ln = ln is not None
    in_specs = [pl.BlockSpec((tm, d), lambda i: (i, 0))]
    args = [x]
    if apply_ln:
        in_specs += [pl.BlockSpec((1, d), lambda i: (0, 0))] * 2
        args += [ln[0].reshape(1, d), ln[1].reshape(1, d)]
    for w in ws:
        in_specs.append(pl.BlockSpec(w.shape, lambda i: (0, 0)))
        args.append(w)
    out_shape = [jax.ShapeDtypeStruct((m, w.shape[1]), dt) for w, dt in zip(ws, out_dtypes)]
    out_specs = [pl.BlockSpec((tm, w.shape[1]), lambda i: (i, 0)) for w in ws]
    if apply_ln:
        out_shape.append(jax.ShapeDtypeStruct((m, d), F32))
        out_specs.append(pl.BlockSpec((tm, d), lambda i: (i, 0)))
    return pl.pallas_call(
        functools.partial(_inproj_body, apply_ln=apply_ln, n_out=n_out),
        grid=(m // tm,),
        in_specs=in_specs,
        out_specs=out_specs,
        out_shape=out_shape,
        compiler_params=_params(("parallel",)),
        name="inproj_ln" if apply_ln else "inproj",
    )(*args)


GLA_Q0, GLA_K0 = 0, GLA_HEADS * GLA_DKP
GLA_V0 = 2 * GLA_HEADS * GLA_DKP
GLA_G0 = GLA_V0 + GLA_HEADS * GLA_DVP
GLA_R0 = GLA_G0 + GLA_HEADS * GLA_DVP
GLA_COLS = GLA_R0 + LANE
GLA_SUB = 2 * GLA_CHUNK


def _gla_body(x_ref, w2_ref, gb_ref, ng_ref, o_ref, s_ref, *, tl):
    @pl.when(pl.program_id(1) == 0)
    def _():
        s_ref[...] = jnp.zeros_like(s_ref)

    c = GLA_CHUNK
    row = lax.broadcasted_iota(jnp.int32, (GLA_SUB, GLA_SUB), 0)
    col = lax.broadcasted_iota(jnp.int32, (GLA_SUB, GLA_SUB), 1)
    same_chunk = (row < c) == (col < c)
    causal = same_chunk & (col <= row)
    tri = causal.astype(BF16)
    kw = GLA_HEADS * GLA_DKP
    first_half = lax.broadcasted_iota(jnp.int32, (GLA_SUB, kw), 0) < c
    ng = ng_ref[...]

    for r0 in range(0, tl, GLA_SUB):
        rows = slice(r0, r0 + GLA_SUB)
        glr = x_ref[rows, GLA_R0:GLA_R0 + LANE].astype(BF16)
        xg = _dot(glr, w2_ref[...]) + gb_ref[...]
        log_a = -_softplus(-xg) / GLA_TAU
        bcum = _split_dot_left(tri, log_a, 3)
        blast = jnp.where(first_half, bcum[c - 1:c, :], bcum[2 * c - 1:2 * c, :])
        q = x_ref[rows, GLA_Q0:GLA_Q0 + kw]
        k = x_ref[rows, GLA_K0:GLA_K0 + kw]
        q_t = (q * (GLA_DK ** -0.5) * jnp.exp(bcum)).astype(BF16)
        k_t = (k * jnp.exp(-bcum)).astype(BF16)
        k_end = (k * jnp.exp(blast - bcum)).astype(BF16)
        dec = jnp.exp(blast)
        for h in range(GLA_HEADS):
            ks = slice(h * GLA_DKP, (h + 1) * GLA_DKP)
            vs = slice(GLA_V0 + h * GLA_DVP, GLA_V0 + (h + 1) * GLA_DVP)
            gs = slice(GLA_G0 + h * GLA_DVP, GLA_G0 + (h + 1) * GLA_DVP)
            v32 = x_ref[rows, vs]
            vb = v32.astype(BF16)
            sc = jnp.where(causal, _dot_nt(q_t[:, ks], k_t[:, ks]), 0.0)
            o_intra = _dot(sc.astype(BF16), vb)
            parts = []
            for j in range(2):
                cr = slice(j * c, (j + 1) * c)
                s_t = s_ref[h]
                o_inter = _dot_nt(q_t[cr, ks], s_t.astype(BF16))
                v_t = v32[cr, :].T.astype(BF16)
                ds_t = _dot(v_t, k_end[cr, ks])
                s_ref[h] = s_t * dec[j * c:j * c + 1, ks] + ds_t
                parts.append(o_intra[cr, :] + o_inter)
            o = jnp.concatenate(parts, axis=0)
            ms = jnp.sum(o * o, axis=-1, keepdims=True) * (1.0 / GLA_DV)
            on = o * lax.rsqrt(ms + LN_EPS) * ng
            g = x_ref[rows, gs]
            o_ref[rows, h * GLA_DVP:(h + 1) * GLA_DVP] = (on * (g / (1.0 + jnp.exp(-g)))).astype(o_ref.dtype)


def _gla(xg, w2p, gbp, ngp, batch, seq, tl=256):
    nt = seq // tl
    return pl.pallas_call(
        functools.partial(_gla_body, tl=tl),
        grid=(batch, nt),
        in_specs=[
            pl.BlockSpec((tl, GLA_COLS), lambda b, t: (b * nt + t, 0)),
            pl.BlockSpec(w2p.shape, lambda b, t: (0, 0)),
            pl.BlockSpec(gbp.shape, lambda b, t: (0, 0)),
            pl.BlockSpec(ngp.shape, lambda b, t: (0, 0)),
        ],
        out_specs=pl.BlockSpec((tl, GLA_HEADS * GLA_DVP), lambda b, t: (b * nt + t, 0)),
        out_shape=jax.ShapeDtypeStruct((batch * seq, GLA_HEADS * GLA_DVP), BF16),
        scratch_shapes=[pltpu.VMEM((GLA_HEADS, GLA_DVP, GLA_DKP), F32)],
        compiler_params=_params(("arbitrary", "arbitrary")),
        name="gla",
    )(xg, w2p, gbp, ngp)


SB_TQ = 128
SB_CK = 128


def _sb_body(q_ref, k_ref, v_ref, o_ref, acc_ref, run_ref):
    i = pl.program_id(1)
    row = lax.broadcasted_iota(jnp.int32, (SB_TQ, SB_CK), 0)
    col = lax.broadcasted_iota(jnp.int32, (SB_TQ, SB_CK), 1)
    tri = (row > col).astype(BF16)
    cmr = col - row
    acc_ref[...] = jnp.zeros_like(acc_ref)
    run_ref[...] = jnp.zeros_like(run_ref)

    def body(step, carry):
        c = i - step
        k0 = pl.multiple_of(c * SB_CK, SB_CK)
        valid = cmr < jnp.where(c < i, SB_CK, 0)
        for h in range(SB_HEADS):
            hs = slice(h * SB_DH, (h + 1) * SB_DH)
            z = _dot_nt(q_ref[:, hs], k_ref[pl.ds(k0, SB_CK), hs]) * (SB_DH ** -0.5)
            sp = _softplus(z)
            l1 = jnp.where(valid, -sp, 0.0)
            suf = _split_dot(l1, tri, 2)
            run = run_ref[h]
            w = jnp.where(valid, jnp.exp(z - sp + suf + run), 0.0)
            acc_ref[:, hs] += _dot(w.astype(BF16), v_ref[pl.ds(k0, SB_CK), hs])
            run_ref[h] = run + jnp.sum(l1, axis=1, keepdims=True)
        return carry

    lax.fori_loop(0, i + 1, body, 0)
    o_ref[...] = acc_ref[...].astype(o_ref.dtype)


def _sb(q, k, v, batch, seq):
    nq = seq // SB_TQ
    w = SB_HEADS * SB_DH
    return pl.pallas_call(
        _sb_body,
        grid=(batch, nq),
        in_specs=[
            pl.BlockSpec((SB_TQ, w), lambda b, i: (b * nq + i, 0)),
            pl.BlockSpec((seq, w), lambda b, i: (b, 0)),
            pl.BlockSpec((seq, w), lambda b, i: (b, 0)),
        ],
        out_specs=pl.BlockSpec((SB_TQ, w), lambda b, i: (b * nq + i, 0)),
        out_shape=jax.ShapeDtypeStruct((batch * seq, w), BF16),
        scratch_shapes=[pltpu.VMEM((SB_TQ, w), F32), pltpu.VMEM((SB_HEADS, SB_TQ, LANE), F32)],
        compiler_params=_params(("parallel", "arbitrary")),
        name="sb",
    )(q, k, v)


DSA_TQ = 128
DSA_CK = 128
DKV_IK0, DKV_K0, DKV_V0 = 0, IDX_DIM, IDX_DIM + DSA_DH
DKV_COLS = DKV_V0 + DSA_LATENT


def _dsa_body(iq_ref, dq_ref, iw_ref, kv_ref, wuv_ref, o_ref,
              isc_ref, thr_ref, tie_ref, qs_ref, m_ref, l_ref, acc_ref, *, seq, topk):
    tq, ck = DSA_TQ, DSA_CK
    i = pl.program_id(1)
    nch = (i + 1) * (tq // ck)
    row = lax.broadcasted_iota(jnp.int32, (tq, ck), 0)
    col = lax.broadcasted_iota(jnp.int32, (tq, ck), 1)
    qpos = i * tq + row
    rowpos = i * tq + lax.broadcasted_iota(jnp.int32, (tq, 1), 0)

    iw = iw_ref[...] * ((IDX_HEADS ** -0.5) * (IDX_DIM ** -0.5))
    iw_cols = [iw[:, h:h + 1] for h in range(IDX_HEADS)]

    def index_chunk(c, carry):
        k0 = pl.multiple_of(c * ck, ck)
        ik = kv_ref[pl.ds(k0, ck), DKV_IK0:DKV_IK0 + IDX_DIM]
        acc = jnp.zeros((tq, ck), F32)
        for h in range(IDX_HEADS):
            rel = jnp.maximum(_dot_nt(iq_ref[:, h * IDX_DIM:(h + 1) * IDX_DIM], ik), 0.0)
            acc = acc + rel * iw_cols[h]
        isc_ref[:, pl.ds(k0, ck)] = jnp.where(k0 + col <= qpos, acc, NEG_INF)
        return carry

    lax.fori_loop(0, nch, index_chunk, 0)

    thr_ref[...] = jnp.full(thr_ref.shape, NEG_INF, F32)
    tie_ref[...] = jnp.where(rowpos < topk, -1, seq).astype(jnp.int32)

    def count(pred):
        def chunk(c, a):
            k0 = pl.multiple_of(c * ck, ck)
            return a + pred(isc_ref[:, pl.ds(k0, ck)], k0).astype(F32)
        a = lax.fori_loop(0, nch, chunk, jnp.zeros((tq, ck), F32))
        return jnp.sum(a, axis=1, keepdims=True)

    def key_to_float(u):
        sk = u ^ INT_MIN
        return pltpu.bitcast(jnp.where(sk >= 0, sk, sk ^ 0x7FFFFFFF), F32)

    @pl.when((i + 1) * tq > topk)
    def _search():
        def bit_step(it, cur):
            cand = cur | lax.shift_left(jnp.int32(1), 31 - it)
            cf = key_to_float(cand)
            cnt = count(lambda blk, k0: blk >= cf)
            return jnp.where(cnt >= topk, cand, cur)

        cur = lax.fori_loop(0, 32, bit_step, jnp.zeros((tq, 1), jnp.int32))
        full = rowpos >= topk
        thr = jnp.where(full, key_to_float(cur), NEG_INF)
        thr_ref[...] = thr
        n_gt = count(lambda blk, k0: blk > thr)
        n_eq = count(lambda blk, k0: blk == thr)
        need = topk - n_gt
        excess = jnp.max(jnp.where(full, n_eq - need, 0.0))

        @pl.when(excess > 0.0)
        def _ties():
            nbits = (seq - 1).bit_length()

            def pos_step(it, p):
                cand = p | lax.shift_left(jnp.int32(1), nbits - 1 - it)
                cnt = count(lambda blk, k0: (blk == thr) & (k0 + col < cand))
                return jnp.where(cnt < need, cand, p)

            p = lax.fori_loop(0, nbits, pos_step, jnp.zeros((tq, 1), jnp.int32))
            tie_ref[...] = jnp.where(full, p, -1)

    for h in range(DSA_HEADS):
        qs_ref[h * tq:(h + 1) * tq, :] = dq_ref[:, h * DSA_DH:(h + 1) * DSA_DH]
    m_ref[...] = jnp.full(m_ref.shape, NEG_INF, F32)
    l_ref[...] = jnp.zeros_like(l_ref)
    acc_ref[...] = jnp.zeros_like(acc_ref)
    thr = thr_ref[...]
    tie = tie_ref[...]

    def attend(c, carry):
        k0 = pl.multiple_of(c * ck, ck)
        kc = kv_ref[pl.ds(k0, ck), DKV_K0:DKV_K0 + DSA_DH]
        vc = kv_ref[pl.ds(k0, ck), DKV_V0:DKV_V0 + DSA_LATENT]
        s5 = _dot_nt(qs_ref[...], kc) * (DSA_DH ** -0.5)
        blk = isc_ref[:, pl.ds(k0, ck)]
        kpos = k0 + col
        sel = (blk > thr) | ((blk == thr) & (kpos <= tie))
        bias = jnp.where(sel, 0.0, NEG_INF)
        dist = (qpos - kpos).astype(F32)
        for h in range(DSA_HEADS):
            hr = slice(h * tq, (h + 1) * tq)
            s = s5[hr, :] - ALIBI_SLOPES[h] * dist + bias
            m_old = m_ref[hr, :]
            m_new = jnp.maximum(m_old, jnp.max(s, axis=1, keepdims=True))
            m_safe = jnp.where(m_new == NEG_INF, 0.0, m_new)
            alpha = jnp.exp(m_old - m_safe)
            p = jnp.exp(s - m_safe)
            l_ref[hr, :] = alpha * l_ref[hr, :] + jnp.sum(p, axis=1, keepdims=True)
            acc_ref[hr, :] = alpha * acc_ref[hr, :] + _dot(p.astype(BF16), vc)
            m_ref[hr, :] = m_new
        return carry

    lax.fori_loop(0, nch, attend, 0)

    for h in range(DSA_HEADS):
        hr = slice(h * tq, (h + 1) * tq)
        o_lat = acc_ref[hr, :] / l_ref[hr, :]
        o_ref[:, h * DSA_DH:(h + 1) * DSA_DH] = _dot(o_lat.astype(BF16), wuv_ref[h]).astype(o_ref.dtype)


def _dsa(iq, dq, iw, dkv, wuv, batch, seq):
    tq = DSA_TQ
    nq = seq // tq
    topk = min(DSA_TOPK_MAX, seq // 4)
    return pl.pallas_call(
        functools.partial(_dsa_body, seq=seq, topk=topk),
        grid=(batch, nq),
        in_specs=[
            pl.BlockSpec((tq, iq.shape[1]), lambda b, i: (b * nq + i, 0)),
            pl.BlockSpec((tq, dq.shape[1]), lambda b, i: (b * nq + i, 0)),
            pl.BlockSpec((tq, iw.shape[1]), lambda b, i: (b * nq + i, 0)),
            pl.BlockSpec((seq, DKV_COLS), lambda b, i: (b, 0)),
            pl.BlockSpec(wuv.shape, lambda b, i: (0, 0, 0)),
        ],
        out_specs=pl.BlockSpec((tq, DSA_HEADS * DSA_DH), lambda b, i: (b * nq + i, 0)),
        out_shape=jax.ShapeDtypeStruct((batch * seq, DSA_HEADS * DSA_DH), BF16),
        scratch_shapes=[
            pltpu.VMEM((tq, seq), F32),
            pltpu.VMEM((tq, 1), F32),
            pltpu.VMEM((tq, 1), jnp.int32),
            pltpu.VMEM((DSA_HEADS * tq, DSA_DH), BF16),
            pltpu.VMEM((DSA_HEADS * tq, 1), F32),
            pltpu.VMEM((DSA_HEADS * tq, 1), F32),
            pltpu.VMEM((DSA_HEADS * tq, DSA_LATENT), F32),
        ],
        compiler_params=_params(("parallel", "arbitrary")),
        name="dsa",
    )(iq, dq, iw, dkv, wuv)


def _outproj_body(og_ref, od_ref, os_ref, h_ref, wg_ref, wd_ref, ws_ref, g_ref, b_ref, o_ref):
    f = _dot(og_ref[...], wg_ref[...]) + _dot(od_ref[...], wd_ref[...]) + _dot(os_ref[...], ws_ref[...])
    o_ref[...] = _ln(ALPHA * h_ref[...] + f, g_ref[...], b_ref[...])


def _outproj(og, od, os_, h, wg, wd, ws, g, b, tm=512):
    m, d = h.shape
    row = lambda a: pl.BlockSpec((tm, a.shape[1]), lambda i: (i, 0))
    full = lambda a: pl.BlockSpec(a.shape, lambda i: (0, 0))
    return pl.pallas_call(
        _outproj_body,
        grid=(m // tm,),
        in_specs=[row(og), row(od), row(os_), row(h), full(wg), full(wd), full(ws), full(g), full(b)],
        out_specs=pl.BlockSpec((tm, d), lambda i: (i, 0)),
        out_shape=jax.ShapeDtypeStruct((m, d), F32),
        compiler_params=_params(("parallel",)),
        name="outproj_ln",
    )(og, od, os_, h, wg, wd, ws, g, b)


def _memkv_body(mem_ref, w_ref, kv_ref):
    kv_ref[...] = _dot(mem_ref[...].astype(BF16), w_ref[...]).astype(kv_ref.dtype)


def _memkv(mem2d, wkv, tm=512):
    m, d = mem2d.shape
    n = wkv.shape[1]
    return pl.pallas_call(
        _memkv_body,
        grid=(m // tm,),
        in_specs=[pl.BlockSpec((tm, d), lambda i: (i, 0)), pl.BlockSpec(wkv.shape, lambda i: (0, 0))],
        out_specs=pl.BlockSpec((tm, n), lambda i: (i, 0)),
        out_shape=jax.ShapeDtypeStruct((m, n), BF16),
        compiler_params=_params(("parallel",)),
        name="mem_kv",
    )(mem2d, wkv)


def _memattn_body(h_ref, kv_ref, wq_ref, wo_ref, g_ref, b_ref, o_ref, ctx_ref):
    d = h_ref.shape[1]
    dh = d // MEM_HEADS
    h = h_ref[...]
    q = _dot(h.astype(BF16), wq_ref[...]).astype(BF16)
    for hd in range(MEM_HEADS):
        cs = slice(hd * dh, (hd + 1) * dh)
        s = _dot_nt(q[:, cs], kv_ref[:, cs]) * (dh ** -0.5)
        s = s - jnp.max(s, axis=-1, keepdims=True)
        p = jnp.exp(s)
        p = p / jnp.sum(p, axis=-1, keepdims=True)
        ctx_ref[:, cs] = _dot(p.astype(BF16), kv_ref[:, d + hd * dh:d + (hd + 1) * dh]).astype(BF16)
    f = _dot(ctx_ref[...], wo_ref[...])
    o_ref[...] = _ln(ALPHA * h + f, g_ref[...], b_ref[...])


def _memattn(h, kv, wq, wo, g, b, batch, seq, n_mem, tm=512):
    m, d = h.shape
    nt = seq // tm
    full = lambda a: pl.BlockSpec(a.shape, lambda bb, t: (0, 0))
    return pl.pallas_call(
        _memattn_body,
        grid=(batch, nt),
        in_specs=[
            pl.BlockSpec((tm, d), lambda bb, t: (bb * nt + t, 0)),
            pl.BlockSpec((n_mem, 2 * d), lambda bb, t: (bb, 0)),
            full(wq), full(wo), full(g), full(b),
        ],
        out_specs=pl.BlockSpec((tm, d), lambda bb, t: (bb * nt + t, 0)),
        out_shape=jax.ShapeDtypeStruct((m, d), F32),
        scratch_shapes=[pltpu.VMEM((tm, d), BF16)],
        compiler_params=_params(("parallel", "parallel")),
        name="mem_attn_ln",
    )(h, kv, wq, wo, g, b)


MLP_FC = 512


def _mlp_body(h_ref, wu_ref, bu_ref, wd_ref, bd_ref, g_ref, b_ref, o_ref):
    h = h_ref[...]
    hb = h.astype(BF16)
    f = None
    for c0 in range(0, wu_ref.shape[1], MLP_FC):
        cs = slice(c0, c0 + MLP_FC)
        u = jnp.maximum(_dot(hb, wu_ref[:, cs]) + bu_ref[:, cs], 0.0)
        d = _dot((u * u).astype(BF16), wd_ref[cs, :])
        f = d if f is None else f + d
    o_ref[...] = _ln(ALPHA * h + (f + bd_ref[...]), g_ref[...], b_ref[...])


def _mlp(h, wu, bu, wd, bd, g, b, tm=512):
    m, d = h.shape
    const = lambda a: pl.BlockSpec(a.shape, lambda i: (0, 0), pipeline_mode=pl.Buffered(1))
    return pl.pallas_call(
        _mlp_body,
        grid=(m // tm,),
        in_specs=[pl.BlockSpec((tm, d), lambda i: (i, 0)),
                  const(wu), const(bu), const(wd), const(bd), const(g), const(b)],
        out_specs=pl.BlockSpec((tm, d), lambda i: (i, 0)),
        out_shape=jax.ShapeDtypeStruct((m, d), F32),
        compiler_params=_params(("parallel",)),
        name="mlp_ln",
    )(h, wu, bu, wd, bd, g, b)


def _pad_heads(w, heads, width, padded):
    lead = w.shape[:-1]
    w = w.reshape(lead + (heads, width))
    w = jnp.pad(w, [(0, 0)] * len(lead) + [(0, 0), (0, padded - width)])
    return w.reshape(lead + (heads * padded,))


def _split_w_in(w_in):
    sizes = (GLA_HEADS * GLA_DK, GLA_HEADS * GLA_DK, GLA_HEADS * GLA_DV, GLA_RANK, GLA_HEADS * GLA_DV,
             DSA_HEADS * DSA_DH, DSA_DH, DSA_LATENT, IDX_HEADS * IDX_DIM, IDX_DIM, IDX_HEADS,
             SB_HEADS * SB_DH, SB_HEADS * SB_DH, SB_HEADS * SB_DH)
    parts = []
    o = 0
    for s in sizes:
        parts.append(w_in[:, o:o + s])
        o += s
    gq, gk, gv, glr, gg, dq, dk, dv, iq, ik, iw, sq, sk, sv = parts
    w_gla = jnp.concatenate([
        _pad_heads(gq, GLA_HEADS, GLA_DK, GLA_DKP), _pad_heads(gk, GLA_HEADS, GLA_DK, GLA_DKP),
        _pad_heads(gv, GLA_HEADS, GLA_DV, GLA_DVP), _pad_heads(gg, GLA_HEADS, GLA_DV, GLA_DVP),
        jnp.pad(glr, ((0, 0), (0, LANE - GLA_RANK)))], axis=1)
    w_dkv = jnp.concatenate([ik, dk, dv], axis=1)
    w_iw = jnp.pad(iw, ((0, 0), (0, LANE - IDX_HEADS)))
    return [w.astype(BF16) for w in (w_gla, iq, dq, w_dkv, w_iw, sq, sk, sv)]


_INPROJ_DTYPES = (F32, BF16, BF16, BF16, F32, BF16, BF16, BF16)


def kernel(x, mem, ln_in_g, ln_in_b, w_in, gla_gate_w2, gla_gate_b, gla_norm_g, dsa_w_uv, w_out,
           ln_mix_g, ln_mix_b, w_mem_q, w_mem_kv, w_mem_o, ln_mem_g, ln_mem_b,
           w_up, b_up, w_down, b_down, ln_ffn_g, ln_ffn_b):
    batch, seq, d = x.shape
    n_mem = mem.shape[1]
    depth = w_in.shape[0]
    x2 = x.reshape(batch * seq, d)
    mem2 = mem.reshape(batch * n_mem, d)
    row = lambda a: a.reshape(1, -1)
    gw, dw = GLA_HEADS * GLA_DV, DSA_HEADS * DSA_DH

    h = None
    for l in range(depth):
        ws = _split_w_in(w_in[l])
        if l == 0:
            outs = _inproj(x2, ws, _INPROJ_DTYPES, ln=(ln_in_g, ln_in_b))
            h = outs[-1]
            outs = outs[:-1]
        else:
            outs = _inproj(h, ws, _INPROJ_DTYPES)
        p_gla, p_iq, p_dq, p_dkv, p_iw, p_sq, p_sk, p_sv = outs

        w2p = jnp.pad(_pad_heads(gla_gate_w2[l], GLA_HEADS, GLA_DK, GLA_DKP),
                      ((0, LANE - GLA_RANK), (0, 0))).astype(BF16)
        gbp = row(_pad_heads(gla_gate_b[l], GLA_HEADS, GLA_DK, GLA_DKP))
        ngp = row(jnp.pad(gla_norm_g[l], (0, GLA_DVP - GLA_DV)))
        o_gla = _gla(p_gla, w2p, gbp, ngp, batch, seq)
        o_dsa = _dsa(p_iq, p_dq, p_iw, p_dkv, dsa_w_uv[l].astype(BF16), batch, seq)
        o_sb = _sb(p_sq, p_sk, p_sv, batch, seq)

        wo = w_out[l]
        wg = jnp.pad(wo[:gw].reshape(GLA_HEADS, GLA_DV, d), ((0, 0), (0, GLA_DVP - GLA_DV), (0, 0)))
        wg = wg.reshape(GLA_HEADS * GLA_DVP, d).astype(BF16)
        h = _outproj(o_gla, o_dsa, o_sb, h, wg, wo[gw:gw + dw].astype(BF16), wo[gw + dw:].astype(BF16),
                     row(ln_mix_g[l]), row(ln_mix_b[l]))

        kv = _memkv(mem2, w_mem_kv[l].astype(BF16))
        h = _memattn(h, kv, w_mem_q[l].astype(BF16), w_mem_o[l].astype(BF16),
                     row(ln_mem_g[l]), row(ln_mem_b[l]), batch, seq, n_mem)

        h = _mlp(h, w_up[l].astype(BF16), row(b_up[l]), w_down[l].astype(BF16), row(b_down[l]),
                 row(ln_ffn_g[l]), row(ln_ffn_b[l]))
    return h.reshape(batch, seq, d)
```

```python
import functools

import jax
import jax.numpy as jnp
from jax import lax
from jax.experimental import pallas as pl
from jax.experimental.pallas import tpu as pltpu

F32 = jnp.float32
BF16 = jnp.bfloat16

DEPTH = 2
LN_EPS = 1e-5
GLA_HEADS, GLA_DK, GLA_DV, GLA_RANK, GLA_TAU, GLA_CHUNK = 4, 48, 96, 16, 16.0, 64
GLA_DKP, GLA_DVP = 64, 128
DSA_HEADS, DSA_DH, DSA_LATENT = 5, 64, 128
IDX_HEADS, IDX_DIM, DSA_TOPK_MAX = 8, 64, 256
SB_HEADS, SB_DH = 5, 64
MEM_HEADS = 4
ALPHA = (2.0 * DEPTH) ** 0.25
ALIBI_SLOPES = tuple(2.0 ** (-8.0 * (i + 1) / DSA_HEADS) for i in range(DSA_HEADS))
NEG_INF = float("-inf")
F32_LOWEST = float(jnp.finfo(jnp.float32).min)
INT_MIN = -(2 ** 31)

LANE = 128
SUBLANE = 8
VMEM_LIMIT = 56 * 1024 * 1024


def _dot(a, b):
    return jnp.dot(a, b, preferred_element_type=F32)


def _dot_nt(a, b):
    return lax.dot_general(a, b, (((1,), (1,)), ((), ())), preferred_element_type=F32)


def _ln(x, g, b):
    mu = jnp.mean(x, axis=-1, keepdims=True)
    xc = x - mu
    var = jnp.mean(xc * xc, axis=-1, keepdims=True)
    return xc * lax.rsqrt(var + LN_EPS) * g + b


def _softplus(x):
    return jnp.maximum(x, 0.0) + jnp.log(1.0 + jnp.exp(-jnp.abs(x)))


def _split_dot_left(m, x, terms):
    out = None
    r = x
    for t in range(terms):
        xb = r.astype(BF16)
        d = _dot(m, xb)
        out = d if out is None else out + d
        if t + 1 < terms:
            r = r - xb.astype(F32)
    return out


def _params(sem):
    return pltpu.CompilerParams(dimension_semantics=sem, vmem_limit_bytes=VMEM_LIMIT)


def _inproj_body(*refs, apply_ln, transposed):
    n_out = len(transposed)
    if apply_ln:
        x_ref, g_ref, b_ref = refs[:3]
        rest = refs[3:]
    else:
        x_ref = refs[0]
        rest = refs[1:]
    w_refs = rest[:n_out]
    o_refs = rest[n_out:2 * n_out]
    x = x_ref[...]
    if apply_ln:
        x = _ln(x, g_ref[...], b_ref[...])
        rest[2 * n_out][...] = x
    xb = x.astype(BF16)
    for w_ref, o_ref, tr in zip(w_refs, o_refs, transposed):
        if tr:
            o_ref[...] = _dot_nt(w_ref[...], xb).astype(o_ref.dtype)
            continue
        n = w_ref.shape[1]
        for n0 in range(0, n, 512):
            n1 = min(n0 + 512, n)
            o_ref[:, n0:n1] = _dot(xb, w_ref[:, n0:n1]).astype(o_ref.dtype)


def _inproj(x, ws, out_dtypes, transposed, ln=None, tm=512):
    m, d = x.shape
    apply_ln = ln is not None
    in_specs = [pl.BlockSpec((tm, d), lambda i: (i, 0))]
    args = [x]
    if apply_ln:
        in_specs += [pl.BlockSpec((1, d), lambda i: (0, 0))] * 2
        args += [ln[0].reshape(1, d), ln[1].reshape(1, d)]
    for w in ws:
        in_specs.append(pl.BlockSpec(w.shape, lambda i: (0, 0)))
        args.append(w)
    out_shape, out_specs = [], []
    for w, dt, tr in zip(ws, out_dtypes, transposed):
        if tr:
            out_shape.append(jax.ShapeDtypeStruct((w.shape[0], m), dt))
            out_specs.append(pl.BlockSpec((w.shape[0], tm), lambda i: (0, i)))
        else:
            out_shape.append(jax.ShapeDtypeStruct((m, w.shape[1]), dt))
            out_specs.append(pl.BlockSpec((tm, w.shape[1]), lambda i: (i, 0)))
    if apply_ln:
        out_shape.append(jax.ShapeDtypeStruct((m, d), F32))
        out_specs.append(pl.BlockSpec((tm, d), lambda i: (i, 0)))
    return pl.pallas_call(
        functools.partial(_inproj_body, apply_ln=apply_ln, transposed=tuple(transposed)),
        grid=(m // tm,),
        in_specs=in_specs,
        out_specs=out_specs,
        out_shape=out_shape,
        compiler_params=_params(("parallel",)),
        name="inproj_ln" if apply_ln else "inproj",
    )(*args)


GLA_Q0, GLA_K0 = 0, GLA_HEADS * GLA_DKP
GLA_V0 = 2 * GLA_HEADS * GLA_DKP
GLA_G0 = GLA_V0 + GLA_HEADS * GLA_DVP
GLA_R0 = GLA_G0 + GLA_HEADS * GLA_DVP
GLA_COLS = GLA_R0 + LANE
GLA_SUB = 2 * GLA_CHUNK


def _gla_body(x_ref, w2_ref, gb_ref, ng_ref, o_ref, s_ref, *, tl):
    @pl.when(pl.program_id(1) == 0)
    def _():
        s_ref[...] = jnp.zeros_like(s_ref)

    c = GLA_CHUNK
    row = lax.broadcasted_iota(jnp.int32, (GLA_SUB, GLA_SUB), 0)
    col = lax.broadcasted_iota(jnp.int32, (GLA_SUB, GLA_SUB), 1)
    same_chunk = (row < c) == (col < c)
    causal = same_chunk & (col <= row)
    tri = causal.astype(BF16)
    kw = GLA_HEADS * GLA_DKP
    first_half = lax.broadcasted_iota(jnp.int32, (GLA_SUB, kw), 0) < c
    ng = ng_ref[...]

    for r0 in range(0, tl, GLA_SUB):
        rows = slice(r0, r0 + GLA_SUB)
        glr = x_ref[rows, GLA_R0:GLA_R0 + LANE].astype(BF16)
        xg = _dot(glr, w2_ref[...]) + gb_ref[...]
        log_a = -_softplus(-xg) / GLA_TAU
        bcum = _split_dot_left(tri, log_a, 3)
        blast = jnp.where(first_half, bcum[c - 1:c, :], bcum[2 * c - 1:2 * c, :])
        q = x_ref[rows, GLA_Q0:GLA_Q0 + kw]
        k = x_ref[rows, GLA_K0:GLA_K0 + kw]
        q_t = (q * (GLA_DK ** -0.5) * jnp.exp(bcum)).astype(BF16)
        k_t = (k * jnp.exp(-bcum)).astype(BF16)
        k_end = (k * jnp.exp(blast - bcum)).astype(BF16)
        dec = jnp.exp(blast)
        for h in range(GLA_HEADS):
            ks = slice(h * GLA_DKP, (h + 1) * GLA_DKP)
            vs = slice(GLA_V0 + h * GLA_DVP, GLA_V0 + (h + 1) * GLA_DVP)
            gs = slice(GLA_G0 + h * GLA_DVP, GLA_G0 + (h + 1) * GLA_DVP)
            v32 = x_ref[rows, vs]
            vb = v32.astype(BF16)
            sc = jnp.where(causal, _dot_nt(q_t[:, ks], k_t[:, ks]), 0.0)
            o_intra = _dot(sc.astype(BF16), vb)
            parts = []
            for j in range(2):
                cr = slice(j * c, (j + 1) * c)
                s_t = s_ref[h]
                o_inter = _dot_nt(q_t[cr, ks], s_t.astype(BF16))
                v_t = v32[cr, :].T.astype(BF16)
                ds_t = _dot(v_t, k_end[cr, ks])
                s_ref[h] = s_t * dec[j * c:j * c + 1, ks] + ds_t
                parts.append(o_intra[cr, :] + o_inter)
            o = jnp.concatenate(parts, axis=0)
            ms = jnp.sum(o * o, axis=-1, keepdims=True) * (1.0 / GLA_DV)
            on = o * lax.rsqrt(ms + LN_EPS) * ng
            g = x_ref[rows, gs]
            o_ref[rows, h * GLA_DVP:(h + 1) * GLA_DVP] = (on * (g / (1.0 + jnp.exp(-g)))).astype(o_ref.dtype)


def _gla(xg, w2p, gbp, ngp, batch, seq, tl=256):
    nt = seq // tl
    return pl.pallas_call(
        functools.partial(_gla_body, tl=tl),
        grid=(batch, nt),
        in_specs=[
            pl.BlockSpec((tl, GLA_COLS), lambda b, t: (b * nt + t, 0)),
            pl.BlockSpec(w2p.shape, lambda b, t: (0, 0)),
            pl.BlockSpec(gbp.shape, lambda b, t: (0, 0)),
            pl.BlockSpec(ngp.shape, lambda b, t: (0, 0)),
        ],
        out_specs=pl.BlockSpec((tl, GLA_HEADS * GLA_DVP), lambda b, t: (b * nt + t, 0)),
        out_shape=jax.ShapeDtypeStruct((batch * seq, GLA_HEADS * GLA_DVP), BF16),
        scratch_shapes=[pltpu.VMEM((GLA_HEADS, GLA_DVP, GLA_DKP), F32)],
        compiler_params=_params(("arbitrary", "arbitrary")),
        name="gla",
    )(xg, w2p, gbp, ngp)


SB_TQ = 256
SB_CK = 128
SB_W = SB_HEADS * SB_DH
SB_WP = 384


def _sb_body(q_ref, k_ref, vt_ref, o_ref, qs_ref, acc_ref, run_ref):
    tq, ck = SB_TQ, SB_CK
    i = pl.program_id(1)
    c_last = (i + 1) * (tq // ck) - 1
    kloc = lax.broadcasted_iota(jnp.int32, (ck, tq), 0)
    qloc = lax.broadcasted_iota(jnp.int32, (ck, tq), 1)
    qpos = i * tq + qloc
    later = (lax.broadcasted_iota(jnp.int32, (ck, ck), 1) > lax.broadcasted_iota(jnp.int32, (ck, ck), 0)).astype(BF16)
    for h in range(SB_HEADS):
        qs_ref[h] = q_ref[:, h * SB_DH:(h + 1) * SB_DH] * (SB_DH ** -0.5)
    acc_ref[...] = jnp.zeros_like(acc_ref)
    run_ref[...] = jnp.zeros_like(run_ref)

    heads = range(SB_HEADS)
    hsl = [slice(h * SB_DH, (h + 1) * SB_DH) for h in heads]

    def chunk(c, masked):
        k0 = pl.multiple_of(c * ck, ck)
        zs = [_dot_nt(k_ref[pl.ds(k0, ck), hsl[h]], qs_ref[h]) for h in heads]
        runs = run_ref[...]
        sps = [_softplus(z) for z in zs]
        l1s = [-sp for sp in sps]
        if masked:
            valid = k0 + kloc < qpos
            l1s = [jnp.where(valid, l1, 0.0) for l1 in l1s]
        suf = _split_dot_left(later, jnp.concatenate(l1s, axis=1), 2)
        new_runs = []
        for h in heads:
            w = jnp.exp(zs[h] - sps[h] + suf[:, h * tq:(h + 1) * tq] + runs[h:h + 1, :])
            if masked:
                w = jnp.where(valid, w, 0.0)
            acc_ref[hsl[h], :] += _dot(vt_ref[hsl[h], pl.ds(k0, ck)], w.astype(BF16))
            new_runs.append(runs[h:h + 1, :] + jnp.sum(l1s[h], axis=0, keepdims=True))
        new_runs = jnp.concatenate(new_runs, axis=0)
        run_ref[0:SB_HEADS, :] = new_runs
        return (jnp.max(jnp.exp(new_runs + 2.0)) > 0.0).astype(jnp.int32)

    n_diag = tq // ck
    alive = jnp.int32(1)
    for j in range(n_diag):
        alive = chunk(c_last - j, True)

    def cond(st):
        step, alive = st
        return (step <= c_last) & (alive > 0)

    def body(st):
        step, _ = st
        return step + 1, chunk(c_last - step, False)

    lax.while_loop(cond, body, (jnp.int32(n_diag), alive))
    o_ref[...] = acc_ref[...].T[:, :SB_W].astype(o_ref.dtype)


def _sb(q, k, vt, batch, seq):
    tq = SB_TQ
    nq = seq // tq
    return pl.pallas_call(
        _sb_body,
        grid=(batch, nq),
        in_specs=[
            pl.BlockSpec((tq, SB_W), lambda b, i: (b * nq + i, 0)),
            pl.BlockSpec((seq, SB_W), lambda b, i: (b, 0)),
            pl.BlockSpec((SB_W, seq), lambda b, i: (0, b)),
        ],
        out_specs=pl.BlockSpec((tq, SB_W), lambda b, i: (b * nq + i, 0)),
        out_shape=jax.ShapeDtypeStruct((batch * seq, SB_W), BF16),
        scratch_shapes=[
            pltpu.VMEM((SB_HEADS, tq, SB_DH), BF16),
            pltpu.VMEM((SB_WP, tq), F32),
            pltpu.VMEM((SUBLANE, tq), F32),
        ],
        compiler_params=_params(("parallel", "arbitrary")),
        name="sb",
    )(q, k, vt)


DSA_TQ = 256
DSA_CK = 256
DSA_NACC = 4
DSA_W = DSA_HEADS * DSA_DH
DSA_WP = 384
DKK_IK0, DKK_K0 = 0, IDX_DIM
DKK_COLS = IDX_DIM + DSA_DH


def _dsa_body(iq_ref, dq_ref, iw_ref, kk_ref, vt_ref, wuvt_ref, o_ref,
              isc_ref, thr_ref, iqs_ref, iwt_ref, qs_ref, kbias_ref, m_ref, l_ref, acc_ref, ot_ref,
              *, seq, topk):
    tq, ck = DSA_TQ, DSA_CK
    grp = ck // SUBLANE
    i = pl.program_id(1)
    nch = (i + 1) * (tq // ck)
    kloc = lax.broadcasted_iota(jnp.int32, (ck, tq), 0)
    qloc = lax.broadcasted_iota(jnp.int32, (ck, tq), 1)
    qpos = i * tq + qloc
    rowpos = i * tq + lax.broadcasted_iota(jnp.int32, (1, tq), 1)
    full = rowpos >= topk

    for h in range(IDX_HEADS):
        iqs_ref[h] = iq_ref[:, h * IDX_DIM:(h + 1) * IDX_DIM]
    for h in range(DSA_HEADS):
        qs_ref[h * tq:(h + 1) * tq, :] = dq_ref[:, h * DSA_DH:(h + 1) * DSA_DH] * (DSA_DH ** -0.5)
    iwt_ref[...] = (iw_ref[...] * ((IDX_HEADS ** -0.5) * (IDX_DIM ** -0.5))).T[0:IDX_HEADS, :]

    def index_chunk(c, carry):
        k0 = pl.multiple_of(c * ck, ck)
        ik = kk_ref[pl.ds(k0, ck), DKK_IK0:DKK_IK0 + IDX_DIM]
        acc = jnp.zeros((ck, tq), F32)
        for h in range(IDX_HEADS):
            rel = jnp.maximum(_dot_nt(ik, iqs_ref[h]), 0.0)
            acc = acc + rel * iwt_ref[h:h + 1, :]
        isc_ref[pl.ds(k0, ck), :] = jnp.where(k0 + kloc <= qpos, acc, NEG_INF)
        return carry

    lax.fori_loop(0, nch, index_chunk, 0)

    thr_ref[...] = jnp.full(thr_ref.shape, F32_LOWEST, F32)

    def count(pred):
        def chunk(c, a):
            k0 = pl.multiple_of(c * ck, ck)
            blk = isc_ref[pl.ds(k0, ck), :].reshape(grp, SUBLANE, tq)
            hit = pred(blk, k0).astype(F32).reshape(grp // DSA_NACC, DSA_NACC, SUBLANE, tq)
            return a + jnp.sum(hit, axis=0)
        a = lax.fori_loop(0, nch, chunk, jnp.zeros((DSA_NACC, SUBLANE, tq), F32))
        return jnp.sum(a.reshape(DSA_NACC * SUBLANE, tq), axis=0, keepdims=True)

    def key_to_float(u):
        sk = u ^ INT_MIN
        return pltpu.bitcast(jnp.where(sk >= 0, sk, sk ^ 0x7FFFFFFF), F32)

    def rows8(v):
        return jnp.broadcast_to(v, (SUBLANE, tq))[None]

    @pl.when((i + 1) * tq > topk)
    def _search():
        def bit_step(it, st):
            cur, n_ge = st
            cand = cur | lax.shift_left(jnp.int32(1), 31 - it)
            cf = rows8(key_to_float(cand))
            cnt = count(lambda blk, k0: blk >= cf)
            ok = cnt >= topk
            return jnp.where(ok, cand, cur), jnp.where(ok, cnt, n_ge)

        cur, n_ge = lax.fori_loop(0, 32, bit_step,
                                  (jnp.zeros((1, tq), jnp.int32), jnp.zeros((1, tq), F32)))
        thr = jnp.where(full, key_to_float(cur), F32_LOWEST)
        thr_ref[...] = thr
        excess = jnp.max(jnp.where(full, n_ge - topk, 0.0))

        @pl.when(excess > 0.0)
        def _ties():
            thr8 = rows8(thr)
            need = topk - count(lambda blk, k0: blk > thr8)
            nbits = (seq - 1).bit_length()
            kpos3 = lax.broadcasted_iota(jnp.int32, (grp, SUBLANE, tq), 0) * SUBLANE + \
                lax.broadcasted_iota(jnp.int32, (grp, SUBLANE, tq), 1)

            def pos_step(it, p):
                cand = p | lax.shift_left(jnp.int32(1), nbits - 1 - it)
                cand8 = rows8(cand)
                cnt = count(lambda blk, k0: (blk == thr8) & (k0 + kpos3 < cand8))
                return jnp.where(cnt < need, cand, p)

            last = lax.fori_loop(0, nbits, pos_step, jnp.zeros((1, tq), jnp.int32))

            def drop(c, carry):
                k0 = pl.multiple_of(c * ck, ck)
                blk = isc_ref[pl.ds(k0, ck), :]
                kill = (blk == thr) & (k0 + kloc > last) & full
                isc_ref[pl.ds(k0, ck), :] = jnp.where(kill, NEG_INF, blk)
                return carry

            lax.fori_loop(0, nch, drop, 0)

    m_ref[...] = jnp.full(m_ref.shape, NEG_INF, F32)
    l_ref[...] = jnp.zeros_like(l_ref)
    acc_ref[...] = jnp.zeros_like(acc_ref)
    thr = thr_ref[...]
    klocf = kloc.astype(F32)
    for h in range(DSA_HEADS):
        kbias_ref[:, h * tq:(h + 1) * tq] = ALIBI_SLOPES[h] * klocf
    slope_row = jnp.concatenate([jnp.full((1, tq), sl, F32) for sl in ALIBI_SLOPES], axis=1)

    def attend(c, carry):
        k0 = pl.multiple_of(c * ck, ck)
        s5 = _dot_nt(kk_ref[pl.ds(k0, ck), DKK_K0:DKK_K0 + DSA_DH], qs_ref[...])
        s5 = s5 + kbias_ref[...]
        sel = isc_ref[pl.ds(k0, ck), :] >= thr
        s5 = jnp.concatenate([jnp.where(sel, s5[:, h * tq:(h + 1) * tq], NEG_INF)
                              for h in range(DSA_HEADS)], axis=1)
        cvec = slope_row * k0.astype(F32)
        m_old = m_ref[...]
        m_new = jnp.maximum(m_old, jnp.max(s5, axis=0, keepdims=True) + cvec)
        m_safe = jnp.where(m_new == NEG_INF, 0.0, m_new)
        alpha = jnp.exp(m_old - m_safe)
        p = jnp.exp(s5 - (m_safe - cvec))
        l_ref[...] = alpha * l_ref[...] + jnp.sum(p, axis=0, keepdims=True)
        acc_ref[...] = alpha * acc_ref[...] + _dot(vt_ref[:, pl.ds(k0, ck)], p.astype(BF16))
        m_ref[...] = m_new
        return carry

    lax.fori_loop(0, nch, attend, 0)

    ot_ref[DSA_W:, :] = jnp.zeros((DSA_WP - DSA_W, tq), F32)
    for h in range(DSA_HEADS):
        hs = slice(h * tq, (h + 1) * tq)
        o_lat = acc_ref[:, hs] / l_ref[:, hs]
        ot_ref[h * DSA_DH:(h + 1) * DSA_DH, :] = _dot(wuvt_ref[h], o_lat.astype(BF16))
    o_ref[...] = ot_ref[...].T[:, :DSA_W].astype(o_ref.dtype)


def _dsa(iq, dq, iw, dkk, dvt, wuvt, batch, seq):
    tq = DSA_TQ
    nq = seq // tq
    topk = min(DSA_TOPK_MAX, seq // 4)
    return pl.pallas_call(
        functools.partial(_dsa_body, seq=seq, topk=topk),
        grid=(batch, nq),
        in_specs=[
            pl.BlockSpec((tq, iq.shape[1]), lambda b, i: (b * nq + i, 0)),
            pl.BlockSpec((tq, dq.shape[1]), lambda b, i: (b * nq + i, 0)),
            pl.BlockSpec((tq, iw.shape[1]), lambda b, i: (b * nq + i, 0)),
            pl.BlockSpec((seq, DKK_COLS), lambda b, i: (b, 0)),
            pl.BlockSpec((DSA_LATENT, seq), lambda b, i: (0, b)),
            pl.BlockSpec(wuvt.shape, lambda b, i: (0, 0, 0)),
        ],
        out_specs=pl.BlockSpec((tq, DSA_W), lambda b, i: (b * nq + i, 0)),
        out_shape=jax.ShapeDtypeStruct((batch * seq, DSA_W), BF16),
        scratch_shapes=[
            pltpu.VMEM((seq, tq), F32),
            pltpu.VMEM((1, tq), F32),
            pltpu.VMEM((IDX_HEADS, tq, IDX_DIM), BF16),
            pltpu.VMEM((IDX_HEADS, tq), F32),
            pltpu.VMEM((DSA_HEADS * tq, DSA_DH), BF16),
            pltpu.VMEM((DSA_CK, DSA_HEADS * tq), F32),
            pltpu.VMEM((1, DSA_HEADS * tq), F32),
            pltpu.VMEM((1, DSA_HEADS * tq), F32),
            pltpu.VMEM((DSA_LATENT, DSA_HEADS * tq), F32),
            pltpu.VMEM((DSA_WP, tq), F32),
        ],
        compiler_params=_params(("parallel", "arbitrary")),
        name="dsa",
    )(iq, dq, iw, dkk, dvt, wuvt)


def _outproj_body(og_ref, od_ref, os_ref, h_ref, wg_ref, wd_ref, ws_ref, g_ref, b_ref, o_ref):
    f = _dot(og_ref[...], wg_ref[...]) + _dot(od_ref[...], wd_ref[...]) + _dot(os_ref[...], ws_ref[...])
    o_ref[...] = _ln(ALPHA * h_ref[...] + f, g_ref[...], b_ref[...])


def _outproj(og, od, os_, h, wg, wd, ws, g, b, tm=512):
    m, d = h.shape
    row = lambda a: pl.BlockSpec((tm, a.shape[1]), lambda i: (i, 0))
    full = lambda a: pl.BlockSpec(a.shape, lambda i: (0, 0))
    return pl.pallas_call(
        _outproj_body,
        grid=(m // tm,),
        in_specs=[row(og), row(od), row(os_), row(h), full(wg), full(wd), full(ws), full(g), full(b)],
        out_specs=pl.BlockSpec((tm, d), lambda i: (i, 0)),
        out_shape=jax.ShapeDtypeStruct((m, d), F32),
        compiler_params=_params(("parallel",)),
        name="outproj_ln",
    )(og, od, os_, h, wg, wd, ws, g, b)


def _memkv_body(mem_ref, w_ref, kv_ref):
    kv_ref[...] = _dot(mem_ref[...].astype(BF16), w_ref[...]).astype(kv_ref.dtype)


def _memkv(mem2d, wkv, tm=512):
    m, d = mem2d.shape
    n = wkv.shape[1]
    return pl.pallas_call(
        _memkv_body,
        grid=(m // tm,),
        in_specs=[pl.BlockSpec((tm, d), lambda i: (i, 0)), pl.BlockSpec(wkv.shape, lambda i: (0, 0))],
        out_specs=pl.BlockSpec((tm, n), lambda i: (i, 0)),
        out_shape=jax.ShapeDtypeStruct((m, n), BF16),
        compiler_params=_params(("parallel",)),
        name="mem_kv",
    )(mem2d, wkv)


def _memattn_body(h_ref, kv_ref, wq_ref, wo_ref, g_ref, b_ref, o_ref, ctx_ref):
    d = h_ref.shape[1]
    dh = d // MEM_HEADS
    h = h_ref[...]
    q = _dot(h.astype(BF16), wq_ref[...]).astype(BF16)
    for hd in range(MEM_HEADS):
        cs = slice(hd * dh, (hd + 1) * dh)
        s = _dot_nt(q[:, cs], kv_ref[:, cs]) * (dh ** -0.5)
        s = s - jnp.max(s, axis=-1, keepdims=True)
        p = jnp.exp(s)
        p = p / jnp.sum(p, axis=-1, keepdims=True)
        ctx_ref[:, cs] = _dot(p.astype(BF16), kv_ref[:, d + hd * dh:d + (hd + 1) * dh]).astype(BF16)
    f = _dot(ctx_ref[...], wo_ref[...])
    o_ref[...] = _ln(ALPHA * h + f, g_ref[...], b_ref[...])


def _memattn(h, kv, wq, wo, g, b, batch, seq, n_mem, tm=512):
    m, d = h.shape
    nt = seq // tm
    full = lambda a: pl.BlockSpec(a.shape, lambda bb, t: (0, 0))
    return pl.pallas_call(
        _memattn_body,
        grid=(batch, nt),
        in_specs=[
            pl.BlockSpec((tm, d), lambda bb, t: (bb * nt + t, 0)),
            pl.BlockSpec((n_mem, 2 * d), lambda bb, t: (bb, 0)),
            full(wq), full(wo), full(g), full(b),
        ],
        out_specs=pl.BlockSpec((tm, d), lambda bb, t: (bb * nt + t, 0)),
        out_shape=jax.ShapeDtypeStruct((m, d), F32),
        scratch_shapes=[pltpu.VMEM((tm, d), BF16)],
        compiler_params=_params(("parallel", "parallel")),
        name="mem_attn_ln",
    )(h, kv, wq, wo, g, b)


MLP_FC = 512


def _mlp_body(h_ref, wu_ref, bu_ref, wd_ref, bd_ref, g_ref, b_ref, o_ref):
    h = h_ref[...]
    hb = h.astype(BF16)
    f = None
    for c0 in range(0, wu_ref.shape[1], MLP_FC):
        cs = slice(c0, c0 + MLP_FC)
        u = jnp.maximum(_dot(hb, wu_ref[:, cs]) + bu_ref[:, cs], 0.0)
        d = _dot((u * u).astype(BF16), wd_ref[cs, :])
        f = d if f is None else f + d
    o_ref[...] = _ln(ALPHA * h + (f + bd_ref[...]), g_ref[...], b_ref[...])


def _mlp(h, wu, bu, wd, bd, g, b, tm=512):
    m, d = h.shape
    const = lambda a: pl.BlockSpec(a.shape, lambda i: (0, 0), pipeline_mode=pl.Buffered(1))
    return pl.pallas_call(
        _mlp_body,
        grid=(m // tm,),
        in_specs=[pl.BlockSpec((tm, d), lambda i: (i, 0)),
                  const(wu), const(bu), const(wd), const(bd), const(g), const(b)],
        out_specs=pl.BlockSpec((tm, d), lambda i: (i, 0)),
        out_shape=jax.ShapeDtypeStruct((m, d), F32),
        compiler_params=_params(("parallel",)),
        name="mlp_ln",
    )(h, wu, bu, wd, bd, g, b)


def _pad_heads(w, heads, width, padded):
    lead = w.shape[:-1]
    w = w.reshape(lead + (heads, width))
    w = jnp.pad(w, [(0, 0)] * len(lead) + [(0, 0), (0, padded - width)])
    return w.reshape(lead + (heads * padded,))


def _split_w_in(w_in):
    sizes = (GLA_HEADS * GLA_DK, GLA_HEADS * GLA_DK, GLA_HEADS * GLA_DV, GLA_RANK, GLA_HEADS * GLA_DV,
             DSA_HEADS * DSA_DH, DSA_DH, DSA_LATENT, IDX_HEADS * IDX_DIM, IDX_DIM, IDX_HEADS,
             SB_HEADS * SB_DH, SB_HEADS * SB_DH, SB_HEADS * SB_DH)
    parts = []
    o = 0
    for s in sizes:
        parts.append(w_in[:, o:o + s])
        o += s
    gq, gk, gv, glr, gg, dq, dk, dv, iq, ik, iw, sq, sk, sv = parts
    w_gla = jnp.concatenate([
        _pad_heads(gq, GLA_HEADS, GLA_DK, GLA_DKP), _pad_heads(gk, GLA_HEADS, GLA_DK, GLA_DKP),
        _pad_heads(gv, GLA_HEADS, GLA_DV, GLA_DVP), _pad_heads(gg, GLA_HEADS, GLA_DV, GLA_DVP),
        jnp.pad(glr, ((0, 0), (0, LANE - GLA_RANK)))], axis=1)
    w_dkk = jnp.concatenate([ik, dk], axis=1)
    w_iw = jnp.pad(iw, ((0, 0), (0, LANE - IDX_HEADS)))
    return [w.astype(BF16) for w in (w_gla, iq, dq, w_dkk, dv.T, w_iw, sq, sk, sv.T)]


_INPROJ_DTYPES = (F32, BF16, BF16, BF16, BF16, F32, BF16, BF16, BF16)
_INPROJ_TRANSPOSED = (False, False, False, False, True, False, False, False, True)


def kernel(x, mem, ln_in_g, ln_in_b, w_in, gla_gate_w2, gla_gate_b, gla_norm_g, dsa_w_uv, w_out,
           ln_mix_g, ln_mix_b, w_mem_q, w_mem_kv, w_mem_o, ln_mem_g, ln_mem_b,
           w_up, b_up, w_down, b_down, ln_ffn_g, ln_ffn_b):
    batch, seq, d = x.shape
    n_mem = mem.shape[1]
    depth = w_in.shape[0]
    x2 = x.reshape(batch * seq, d)
    mem2 = mem.reshape(batch * n_mem, d)
    row = lambda a: a.reshape(1, -1)
    gw, dw = GLA_HEADS * GLA_DV, DSA_HEADS * DSA_DH

    h = None
    for l in range(depth):
        ws = _split_w_in(w_in[l])
        if l == 0:
            outs = _inproj(x2, ws, _INPROJ_DTYPES, _INPROJ_TRANSPOSED, ln=(ln_in_g, ln_in_b))
            h = outs[-1]
            outs = outs[:-1]
        else:
            outs = _inproj(h, ws, _INPROJ_DTYPES, _INPROJ_TRANSPOSED)
        p_gla, p_iq, p_dq, p_dkk, p_dvt, p_iw, p_sq, p_sk, p_svt = outs

        w2p = jnp.pad(_pad_heads(gla_gate_w2[l], GLA_HEADS, GLA_DK, GLA_DKP),
                      ((0, LANE - GLA_RANK), (0, 0))).astype(BF16)
        gbp = row(_pad_heads(gla_gate_b[l], GLA_HEADS, GLA_DK, GLA_DKP))
        ngp = row(jnp.pad(gla_norm_g[l], (0, GLA_DVP - GLA_DV)))
        o_gla = _gla(p_gla, w2p, gbp, ngp, batch, seq)
        wuvt = jnp.swapaxes(dsa_w_uv[l], 1, 2).astype(BF16)
        o_dsa = _dsa(p_iq, p_dq, p_iw, p_dkk, p_dvt, wuvt, batch, seq)
        o_sb = _sb(p_sq, p_sk, p_svt, batch, seq)

        wo = w_out[l]
        wg = jnp.pad(wo[:gw].reshape(GLA_HEADS, GLA_DV, d), ((0, 0), (0, GLA_DVP - GLA_DV), (0, 0)))
        wg = wg.reshape(GLA_HEADS * GLA_DVP, d).astype(BF16)
        h = _outproj(o_gla, o_dsa, o_sb, h, wg, wo[gw:gw + dw].astype(BF16), wo[gw + dw:].astype(BF16),
                     row(ln_mix_g[l]), row(ln_mix_b[l]))

        kv = _memkv(mem2, w_mem_kv[l].astype(BF16))
        h = _memattn(h, kv, w_mem_q[l].astype(BF16), w_mem_o[l].astype(BF16),
                     row(ln_mem_g[l]), row(ln_mem_b[l]), batch, seq, n_mem)

        h = _mlp(h, w_up[l].astype(BF16), row(b_up[l]), w_down[l].astype(BF16), row(b_down[l]),
                 row(ln_ffn_g[l]), row(ln_ffn_b[l]))
    return h.reshape(batch, seq, d)
```

```python
import functools

import jax
import jax.numpy as jnp
import numpy as np
from jax import lax
from jax.experimental import pallas as pl
from jax.experimental.pallas import tpu as pltpu

F32 = jnp.float32
BF16 = jnp.bfloat16
HALF = jnp.bfloat16

DEPTH = 2
LN_EPS = 1e-5
GLA_HEADS, GLA_DK, GLA_DV, GLA_RANK, GLA_TAU, GLA_CHUNK = 4, 48, 96, 16, 16.0, 64
GLA_DKP, GLA_DVP = 64, 128
DSA_HEADS, DSA_DH, DSA_LATENT = 5, 64, 128
IDX_HEADS, IDX_DIM, DSA_TOPK_MAX = 8, 64, 256
SB_HEADS, SB_DH = 5, 64
MEM_HEADS = 4
ALPHA = (2.0 * DEPTH) ** 0.25
ALIBI_SLOPES = tuple(2.0 ** (-8.0 * (i + 1) / DSA_HEADS) for i in range(DSA_HEADS))
ALIBI_TERMS = 3
NEG_INF = float("-inf")
F32_LOWEST = float(jnp.finfo(jnp.float32).min)
INT_MIN = -(2 ** 31)

LANE = 128
SUBLANE = 8
PACKED_SUBLANE = 16
VMEM_LIMIT = 56 * 1024 * 1024


def _dot(a, b):
    return jnp.dot(a, b, preferred_element_type=F32)


def _dot_nt(a, b):
    return lax.dot_general(a, b, (((1,), (1,)), ((), ())), preferred_element_type=F32)


def _ln(x, g, b):
    mu = jnp.mean(x, axis=-1, keepdims=True)
    xc = x - mu
    var = jnp.mean(xc * xc, axis=-1, keepdims=True)
    return xc * lax.rsqrt(var + LN_EPS) * g + b


def _softplus(x):
    return jnp.maximum(x, 0.0) + jnp.log(1.0 + jnp.exp(-jnp.abs(x)))


def _split_dot_left(m, x, terms):
    out = None
    r = x
    for t in range(terms):
        xb = r.astype(BF16)
        d = _dot(m, xb)
        out = d if out is None else out + d
        if t + 1 < terms:
            r = r - xb.astype(F32)
    return out


def _bf16_terms(x, n):
    terms = []
    r = np.float32(x)
    for _ in range(n):
        t = np.float32(np.asarray(r).astype(jnp.bfloat16))
        terms.append(float(t))
        r = np.float32(r - t)
    return tuple(terms)


def _params(sem):
    return pltpu.CompilerParams(dimension_semantics=sem, vmem_limit_bytes=VMEM_LIMIT)


def _inproj_body(*refs, apply_ln, transposed):
    n_out = len(transposed)
    if apply_ln:
        x_ref, g_ref, b_ref = refs[:3]
        rest = refs[3:]
    else:
        x_ref = refs[0]
        rest = refs[1:]
    w_refs = rest[:n_out]
    o_refs = rest[n_out:2 * n_out]
    x = x_ref[...]
    if apply_ln:
        x = _ln(x, g_ref[...], b_ref[...])
        rest[2 * n_out][...] = x
    xb = x.astype(BF16)
    for w_ref, o_ref, tr in zip(w_refs, o_refs, transposed):
        if tr:
            o_ref[...] = _dot_nt(w_ref[...], xb).astype(o_ref.dtype)
            continue
        n = w_ref.shape[1]
        for n0 in range(0, n, 512):
            n1 = min(n0 + 512, n)
            o_ref[:, n0:n1] = _dot(xb, w_ref[:, n0:n1]).astype(o_ref.dtype)


def _inproj(x, ws, out_dtypes, transposed, ln=None, tm=512):
    m, d = x.shape
    apply_ln = ln is not None
    in_specs = [pl.BlockSpec((tm, d), lambda i: (i, 0))]
    args = [x]
    if apply_ln:
        in_specs += [pl.BlockSpec((1, d), lambda i: (0, 0))] * 2
        args += [ln[0].reshape(1, d), ln[1].reshape(1, d)]
    for w in ws:
        in_specs.append(pl.BlockSpec(w.shape, lambda i: (0, 0)))
        args.append(w)
    out_shape, out_specs = [], []
    for w, dt, tr in zip(ws, out_dtypes, transposed):
        if tr:
            out_shape.append(jax.ShapeDtypeStruct((w.shape[0], m), dt))
            out_specs.append(pl.BlockSpec((w.shape[0], tm), lambda i: (0, i)))
        else:
            out_shape.append(jax.ShapeDtypeStruct((m, w.shape[1]), dt))
            out_specs.append(pl.BlockSpec((tm, w.shape[1]), lambda i: (i, 0)))
    if apply_ln:
        out_shape.append(jax.ShapeDtypeStruct((m, d), F32))
        out_specs.append(pl.BlockSpec((tm, d), lambda i: (i, 0)))
    return pl.pallas_call(
        functools.partial(_inproj_body, apply_ln=apply_ln, transposed=tuple(transposed)),
        grid=(m // tm,),
        in_specs=in_specs,
        out_specs=out_specs,
        out_shape=out_shape,
        compiler_params=_params(("parallel",)),
        name="inproj_ln" if apply_ln else "inproj",
    )(*args)


GLA_KW = GLA_HEADS * GLA_DKP
GLA_VW = GLA_HEADS * GLA_DVP
GLA_Q0, GLA_K0, GLA_G0 = 0, GLA_KW, 2 * GLA_KW
GLA_COLS = GLA_G0 + GLA_VW
GLA_TL = 256
GLA_NC = GLA_TL // GLA_CHUNK


def _gla_body(x_ref, v_ref, vt_ref, r_ref, w2_ref, gb_ref, ng_ref, o_ref, s_ref):
    @pl.when(pl.program_id(1) == 0)
    def _():
        s_ref[...] = jnp.zeros_like(s_ref)

    tl, c, nc = GLA_TL, GLA_CHUNK, GLA_NC
    row = lax.broadcasted_iota(jnp.int32, (tl, tl), 0)
    col = lax.broadcasted_iota(jnp.int32, (tl, tl), 1)
    rch = jnp.floor(row.astype(F32) * (1.0 / c))
    cch = jnp.floor(col.astype(F32) * (1.0 / c))
    causal = (rch == cch) & (col <= row)
    tri = causal.astype(BF16)
    heads = range(GLA_HEADS)
    ksl = [slice(h * GLA_DKP, (h + 1) * GLA_DKP) for h in heads]
    vsl = [slice(h * GLA_DVP, (h + 1) * GLA_DVP) for h in heads]

    xg = _dot(r_ref[...], w2_ref[...]) + gb_ref[...]
    log_a = -_softplus(-xg) / GLA_TAU
    bcum = _split_dot_left(tri, log_a, 3)
    krow = lax.broadcasted_iota(jnp.int32, (tl, GLA_KW), 0)
    blast = bcum[tl - 1:tl, :]
    for j in range(nc - 2, -1, -1):
        blast = jnp.where(krow < (j + 1) * c, bcum[(j + 1) * c - 1:(j + 1) * c, :], blast)
    q = x_ref[:, GLA_Q0:GLA_Q0 + GLA_KW]
    k = x_ref[:, GLA_K0:GLA_K0 + GLA_KW]
    q_t = (q * (GLA_DK ** -0.5) * jnp.exp(bcum)).astype(BF16)
    k_t = (k * jnp.exp(-bcum)).astype(BF16)
    k_end = (k * jnp.exp(blast - bcum)).astype(BF16)
    dec = [jnp.exp(bcum[(j + 1) * c - 1:(j + 1) * c, :]) for j in range(nc)]

    lane_chunk = jnp.floor(lax.broadcasted_iota(jnp.int32, (GLA_DVP, tl), 1).astype(F32) * (1.0 / c))
    o_intra, ds = [], []
    for h in heads:
        sc = jnp.where(causal, _dot_nt(q_t[:, ksl[h]], k_t[:, ksl[h]]), 0.0)
        o_intra.append(_dot(sc.astype(BF16), v_ref[:, vsl[h]]))
        vt = vt_ref[vsl[h], :]
        zero = jnp.zeros_like(vt)
        vt_by_chunk = jnp.concatenate([jnp.where(lane_chunk == float(j), vt, zero) for j in range(nc)], axis=0)
        ds.append(_dot(vt_by_chunk, k_end[:, ksl[h]]))

    ng = ng_ref[...]
    for h in heads:
        s = s_ref[h]
        states = []
        for j in range(nc):
            states.append(s.astype(BF16))
            s = s * dec[j][:, ksl[h]] + ds[h][j * GLA_DVP:(j + 1) * GLA_DVP, :]
        s_ref[h] = s
        inter_all = _dot_nt(q_t[:, ksl[h]], jnp.concatenate(states, axis=0))
        o_inter = jnp.concatenate(
            [inter_all[j * c:(j + 1) * c, j * GLA_DVP:(j + 1) * GLA_DVP] for j in range(nc)], axis=0)
        o = o_intra[h] + o_inter
        ms = jnp.sum(o * o, axis=-1, keepdims=True) * (1.0 / GLA_DV)
        on = o * lax.rsqrt(ms + LN_EPS) * ng
        g = x_ref[:, GLA_G0 + h * GLA_DVP:GLA_G0 + (h + 1) * GLA_DVP]
        o_ref[:, vsl[h]] = (on * (g / (1.0 + jnp.exp(-g)))).astype(o_ref.dtype)


def _gla(x32, v, vt, glr, w2p, gbp, ngp, batch, seq):
    tl = GLA_TL
    nt = seq // tl
    const = lambda a: pl.BlockSpec(a.shape, lambda b, t: (0, 0))
    return pl.pallas_call(
        _gla_body,
        grid=(batch, nt),
        in_specs=[
            pl.BlockSpec((tl, GLA_COLS), lambda b, t: (b * nt + t, 0)),
            pl.BlockSpec((tl, GLA_VW), lambda b, t: (b * nt + t, 0)),
            pl.BlockSpec((GLA_VW, tl), lambda b, t: (0, b * nt + t)),
            pl.BlockSpec((tl, LANE), lambda b, t: (b * nt + t, 0)),
            const(w2p), const(gbp), const(ngp),
        ],
        out_specs=pl.BlockSpec((tl, GLA_VW), lambda b, t: (b * nt + t, 0)),
        out_shape=jax.ShapeDtypeStruct((batch * seq, GLA_VW), BF16),
        scratch_shapes=[pltpu.VMEM((GLA_HEADS, GLA_DVP, GLA_DKP), F32)],
        compiler_params=_params(("arbitrary", "arbitrary")),
        name="gla",
    )(x32, v, vt, glr, w2p, gbp, ngp)


SB_TQ = 256
SB_CK = 128
SB_W = SB_HEADS * SB_DH
SB_WP = 384


def _sb_body(q_ref, k_ref, vt_ref, o_ref, qs_ref, acc_ref, run_ref):
    tq, ck = SB_TQ, SB_CK
    i = pl.program_id(1)
    c_last = (i + 1) * (tq // ck) - 1
    kloc = lax.broadcasted_iota(jnp.int32, (ck, tq), 0)
    qloc = lax.broadcasted_iota(jnp.int32, (ck, tq), 1)
    qpos = i * tq + qloc
    later = (lax.broadcasted_iota(jnp.int32, (ck, ck), 1) > lax.broadcasted_iota(jnp.int32, (ck, ck), 0)).astype(BF16)
    for h in range(SB_HEADS):
        qs_ref[h] = q_ref[:, h * SB_DH:(h + 1) * SB_DH] * (SB_DH ** -0.5)
    acc_ref[...] = jnp.zeros_like(acc_ref)
    run_ref[...] = jnp.zeros_like(run_ref)

    heads = range(SB_HEADS)
    hsl = [slice(h * SB_DH, (h + 1) * SB_DH) for h in heads]

    def chunk(c, masked):
        k0 = pl.multiple_of(c * ck, ck)
        zs = [_dot_nt(k_ref[pl.ds(k0, ck), hsl[h]], qs_ref[h]) for h in heads]
        runs = run_ref[...]
        sps = [_softplus(z) for z in zs]
        l1s = [-sp for sp in sps]
        if masked:
            valid = k0 + kloc < qpos
            l1s = [jnp.where(valid, l1, 0.0) for l1 in l1s]
        suf = _split_dot_left(later, jnp.concatenate(l1s, axis=1), 2)
        new_runs = []
        for h in heads:
            w = jnp.exp(zs[h] - sps[h] + suf[:, h * tq:(h + 1) * tq] + runs[h:h + 1, :])
            if masked:
                w = jnp.where(valid, w, 0.0)
            acc_ref[hsl[h], :] += _dot(vt_ref[hsl[h], pl.ds(k0, ck)], w.astype(BF16))
            new_runs.append(runs[h:h + 1, :] + jnp.sum(l1s[h], axis=0, keepdims=True))
        new_runs = jnp.concatenate(new_runs, axis=0)
        run_ref[0:SB_HEADS, :] = new_runs
        return (jnp.max(jnp.exp(new_runs + 2.0)) > 0.0).astype(jnp.int32)

    n_diag = tq // ck
    alive = jnp.int32(1)
    for j in range(n_diag):
        alive = chunk(c_last - j, True)

    def cond(st):
        step, alive = st
        return (step <= c_last) & (alive > 0)

    def body(st):
        step, _ = st
        return step + 1, chunk(c_last - step, False)

    lax.while_loop(cond, body, (jnp.int32(n_diag), alive))
    o_ref[...] = acc_ref[...].T[:, :SB_W].astype(o_ref.dtype)


def _sb(q, k, vt, batch, seq):
    tq = SB_TQ
    nq = seq // tq
    return pl.pallas_call(
        _sb_body,
        grid=(batch, nq),
        in_specs=[
            pl.BlockSpec((tq, SB_W), lambda b, i: (b * nq + i, 0)),
            pl.BlockSpec((seq, SB_W), lambda b, i: (b, 0)),
            pl.BlockSpec((SB_W, seq), lambda b, i: (0, b)),
        ],
        out_specs=pl.BlockSpec((tq, SB_W), lambda b, i: (b * nq + i, 0)),
        out_shape=jax.ShapeDtypeStruct((batch * seq, SB_W), BF16),
        scratch_shapes=[
            pltpu.VMEM((SB_HEADS, tq, SB_DH), BF16),
            pltpu.VMEM((SB_WP, tq), F32),
            pltpu.VMEM((SUBLANE, tq), F32),
        ],
        compiler_params=_params(("parallel", "arbitrary")),
        name="sb",
    )(q, k, vt)


DSA_TQ = 256
DSA_CK = 256
DSA_NACC = 4
DSA_W = DSA_HEADS * DSA_DH
DSA_WP = 384
DKK_IK0, DKK_K0 = 0, IDX_DIM
DKK_COLS = IDX_DIM + DSA_DH


def _dsa_body(iq_ref, dq_ref, iw_ref, kk_ref, vt_ref, wuvt_ref, o_ref,
              isc_ref, ihi_ref, thr_ref, iqs_ref, iwt_ref, qs_ref, kaug_ref, sraw_ref,
              m_ref, l_ref, acc_ref, ot_ref,
              *, seq, topk):
    tq, ck = DSA_TQ, DSA_CK
    grp = ck // SUBLANE
    i = pl.program_id(1)
    nch = (i + 1) * (tq // ck)
    kloc = lax.broadcasted_iota(jnp.int32, (ck, tq), 0)
    qloc = lax.broadcasted_iota(jnp.int32, (ck, tq), 1)
    qpos = i * tq + qloc
    rowpos = i * tq + lax.broadcasted_iota(jnp.int32, (1, tq), 1)
    full = rowpos >= topk

    for h in range(IDX_HEADS):
        iqs_ref[h] = iq_ref[:, h * IDX_DIM:(h + 1) * IDX_DIM]
    qlane = lax.broadcasted_iota(jnp.int32, (tq, LANE - DSA_DH), 1)
    for h in range(DSA_HEADS):
        hr = slice(h * tq, (h + 1) * tq)
        qs_ref[hr, 0:DSA_DH] = dq_ref[:, h * DSA_DH:(h + 1) * DSA_DH] * (DSA_DH ** -0.5)
        pieces = jnp.zeros((tq, LANE - DSA_DH), F32)
        for t, piece in enumerate(_bf16_terms(ALIBI_SLOPES[h], ALIBI_TERMS)):
            pieces = jnp.where(qlane == t, piece, pieces)
        qs_ref[hr, DSA_DH:LANE] = pieces.astype(BF16)
    klane = lax.broadcasted_iota(jnp.int32, (ck, LANE - DSA_DH), 1)
    koff = lax.broadcasted_iota(jnp.int32, (ck, LANE - DSA_DH), 0).astype(F32)
    kaug_ref[:, DSA_DH:LANE] = jnp.where(klane < ALIBI_TERMS, koff, 0.0).astype(BF16)
    iwt_ref[...] = (iw_ref[...] * ((IDX_HEADS ** -0.5) * (IDX_DIM ** -0.5))).T[0:IDX_HEADS, :]

    def scores(c):
        k0 = pl.multiple_of(c * ck, ck)
        kaug_ref[:, 0:DSA_DH] = kk_ref[pl.ds(k0, ck), DKK_K0:DKK_K0 + DSA_DH]
        return _dot_nt(kaug_ref[...], qs_ref[...])

    sraw_ref[...] = scores(0)

    def index_chunk(c, carry):
        k0 = pl.multiple_of(c * ck, ck)
        ik = kk_ref[pl.ds(k0, ck), DKK_IK0:DKK_IK0 + IDX_DIM]
        acc = jnp.zeros((ck, tq), F32)
        for h in range(IDX_HEADS):
            rel = jnp.maximum(_dot_nt(ik, iqs_ref[h]), 0.0)
            acc = acc + rel * iwt_ref[h:h + 1, :]
        sc = jnp.where(k0 + kloc <= qpos, acc, NEG_INF)
        isc_ref[pl.ds(k0, ck), :] = sc
        ihi_ref[pl.ds(k0, ck), :] = sc.astype(HALF)
        return carry

    lax.fori_loop(0, nch, index_chunk, 0)

    thr_ref[...] = jnp.full(thr_ref.shape, F32_LOWEST, F32)

    def count(pred):
        def chunk(c, a):
            k0 = pl.multiple_of(c * ck, ck)
            blk = isc_ref[pl.ds(k0, ck), :].reshape(grp, SUBLANE, tq)
            hit = pred(blk, k0).astype(F32).reshape(grp // DSA_NACC, DSA_NACC, SUBLANE, tq)
            return a + jnp.sum(hit, axis=0)
        a = lax.fori_loop(0, nch, chunk, jnp.zeros((DSA_NACC, SUBLANE, tq), F32))
        return jnp.sum(a.reshape(DSA_NACC * SUBLANE, tq), axis=0, keepdims=True)

    def key_to_float(u):
        sk = u ^ INT_MIN
        return pltpu.bitcast(jnp.where(sk >= 0, sk, sk ^ 0x7FFFFFFF), F32)

    def rows8(v):
        return jnp.broadcast_to(v, (SUBLANE, tq))[None]

    def count_high(cand_hi):
        grp16 = ck // PACKED_SUBLANE
        cb = jnp.broadcast_to(cand_hi, (PACKED_SUBLANE, tq))[None]
        one = jnp.ones((), HALF)
        zero = jnp.zeros((), HALF)

        def chunk(c, a):
            k0 = pl.multiple_of(c * ck, ck)
            blk = ihi_ref[pl.ds(k0, ck), :].reshape(grp16, PACKED_SUBLANE, tq)
            hit = jnp.where(blk >= cb, one, zero).reshape(grp16 // DSA_NACC, DSA_NACC, PACKED_SUBLANE, tq)
            for g in range(grp16 // DSA_NACC):
                a = a + hit[g]
            return a
        a = lax.fori_loop(0, nch, chunk, jnp.zeros((DSA_NACC, PACKED_SUBLANE, tq), HALF))
        return jnp.sum(a.astype(F32).reshape(DSA_NACC * PACKED_SUBLANE, tq), axis=0, keepdims=True)

    @pl.when((i + 1) * tq > topk)
    def _search():
        def half_key(u):
            return u | jnp.where((u ^ INT_MIN) < 0, 0xFFFF, 0)

        def coarse_step(it, cur):
            cand = cur | lax.shift_left(jnp.int32(1), 31 - it)
            cnt = count_high(key_to_float(half_key(cand)).astype(HALF))
            return jnp.where(cnt >= topk, cand, cur)

        coarse = lax.fori_loop(0, 16, coarse_step, jnp.zeros((1, tq), jnp.int32))
        base = half_key(coarse) - 0x8000

        def count_ge(u):
            cf = rows8(key_to_float(u))
            return count(lambda blk, k0: blk >= cf)

        def fine_step(it, st):
            off, n_ge = st
            cand = off | lax.shift_left(jnp.int32(1), 16 - it)
            cnt = count_ge(base + cand)
            ok = cnt >= topk
            return jnp.where(ok, cand, off), jnp.where(ok, cnt, n_ge)

        off, n_ge = lax.fori_loop(0, 17, fine_step, (jnp.zeros((1, tq), jnp.int32), count_ge(base)))
        thr = jnp.where(full, key_to_float(base + off), F32_LOWEST)
        thr_ref[...] = thr
        excess = jnp.max(jnp.where(full, n_ge - topk, 0.0))

        @pl.when(excess > 0.0)
        def _ties():
            thr8 = rows8(thr)
            need = topk - count(lambda blk, k0: blk > thr8)
            nbits = (seq - 1).bit_length()
            kpos3 = lax.broadcasted_iota(jnp.int32, (grp, SUBLANE, tq), 0) * SUBLANE + \
                lax.broadcasted_iota(jnp.int32, (grp, SUBLANE, tq), 1)

            def pos_step(it, p):
                cand = p | lax.shift_left(jnp.int32(1), nbits - 1 - it)
                cand8 = rows8(cand)
                cnt = count(lambda blk, k0: (blk == thr8) & (k0 + kpos3 < cand8))
                return jnp.where(cnt < need, cand, p)

            last = lax.fori_loop(0, nbits, pos_step, jnp.zeros((1, tq), jnp.int32))

            def drop(c, carry):
                k0 = pl.multiple_of(c * ck, ck)
                blk = isc_ref[pl.ds(k0, ck), :]
                kill = (blk == thr) & (k0 + kloc > last) & full
                isc_ref[pl.ds(k0, ck), :] = jnp.where(kill, NEG_INF, blk)
                return carry

            lax.fori_loop(0, nch, drop, 0)

    m_ref[...] = jnp.full(m_ref.shape, NEG_INF, F32)
    l_ref[...] = jnp.zeros_like(l_ref)
    acc_ref[...] = jnp.zeros_like(acc_ref)
    thr = thr_ref[...]
    slope_row = jnp.concatenate([jnp.full((1, tq), sl, F32) for sl in ALIBI_SLOPES], axis=1)

    def attend(c, carry):
        k0 = pl.multiple_of(c * ck, ck)
        s5 = sraw_ref[...]
        s_next = scores(jnp.minimum(c + 1, nch - 1))
        sel = isc_ref[pl.ds(k0, ck), :] >= thr
        s5 = jnp.concatenate([jnp.where(sel, s5[:, h * tq:(h + 1) * tq], NEG_INF)
                              for h in range(DSA_HEADS)], axis=1)
        cvec = slope_row * k0.astype(F32)
        m_old = m_ref[...]
        m_new = jnp.maximum(m_old, jnp.max(s5, axis=0, keepdims=True) + cvec)
        m_safe = jnp.where(m_new == NEG_INF, 0.0, m_new)
        alpha = jnp.exp(m_old - m_safe)
        p = jnp.exp(s5 - (m_safe - cvec))
        l_ref[...] = alpha * l_ref[...] + jnp.sum(p, axis=0, keepdims=True)
        acc_ref[...] = alpha * acc_ref[...] + _dot(vt_ref[:, pl.ds(k0, ck)], p.astype(BF16))
        m_ref[...] = m_new
        sraw_ref[...] = s_next
        return carry

    lax.fori_loop(0, nch, attend, 0)

    ot_ref[DSA_W:, :] = jnp.zeros((DSA_WP - DSA_W, tq), F32)
    for h in range(DSA_HEADS):
        hs = slice(h * tq, (h + 1) * tq)
        o_lat = acc_ref[:, hs] / l_ref[:, hs]
        ot_ref[h * DSA_DH:(h + 1) * DSA_DH, :] = _dot(wuvt_ref[h], o_lat.astype(BF16))
    o_ref[...] = ot_ref[...].T[:, :DSA_W].astype(o_ref.dtype)


def _dsa(iq, dq, iw, dkk, dvt, wuvt, batch, seq):
    tq = DSA_TQ
    nq = seq // tq
    topk = min(DSA_TOPK_MAX, seq // 4)
    return pl.pallas_call(
        functools.partial(_dsa_body, seq=seq, topk=topk),
        grid=(batch, nq),
        in_specs=[
            pl.BlockSpec((tq, iq.shape[1]), lambda b, i: (b * nq + i, 0)),
            pl.BlockSpec((tq, dq.shape[1]), lambda b, i: (b * nq + i, 0)),
            pl.BlockSpec((tq, iw.shape[1]), lambda b, i: (b * nq + i, 0)),
            pl.BlockSpec((seq, DKK_COLS), lambda b, i: (b, 0)),
            pl.BlockSpec((DSA_LATENT, seq), lambda b, i: (0, b)),
            pl.BlockSpec(wuvt.shape, lambda b, i: (0, 0, 0)),
        ],
        out_specs=pl.BlockSpec((tq, DSA_W), lambda b, i: (b * nq + i, 0)),
        out_shape=jax.ShapeDtypeStruct((batch * seq, DSA_W), BF16),
        scratch_shapes=[
            pltpu.VMEM((seq, tq), F32),
            pltpu.VMEM((seq, tq), HALF),
            pltpu.VMEM((1, tq), F32),
            pltpu.VMEM((IDX_HEADS, tq, IDX_DIM), BF16),
            pltpu.VMEM((IDX_HEADS, tq), F32),
            pltpu.VMEM((DSA_HEADS * tq, LANE), BF16),
            pltpu.VMEM((DSA_CK, LANE), BF16),
            pltpu.VMEM((DSA_CK, DSA_HEADS * tq), F32),
            pltpu.VMEM((1, DSA_HEADS * tq), F32),
            pltpu.VMEM((1, DSA_HEADS * tq), F32),
            pltpu.VMEM((DSA_LATENT, DSA_HEADS * tq), F32),
            pltpu.VMEM((DSA_WP, tq), F32),
        ],
        compiler_params=_params(("parallel", "arbitrary")),
        name="dsa",
    )(iq, dq, iw, dkk, dvt, wuvt)


def _outproj_body(og_ref, od_ref, os_ref, h_ref, wg_ref, wd_ref, ws_ref, g_ref, b_ref, o_ref):
    f = _dot(og_ref[...], wg_ref[...]) + _dot(od_ref[...], wd_ref[...]) + _dot(os_ref[...], ws_ref[...])
    o_ref[...] = _ln(ALPHA * h_ref[...] + f, g_ref[...], b_ref[...])


def _outproj(og, od, os_, h, wg, wd, ws, g, b, tm=512):
    m, d = h.shape
    row = lambda a: pl.BlockSpec((tm, a.shape[1]), lambda i: (i, 0))
    full = lambda a: pl.BlockSpec(a.shape, lambda i: (0, 0))
    return pl.pallas_call(
        _outproj_body,
        grid=(m // tm,),
        in_specs=[row(og), row(od), row(os_), row(h), full(wg), full(wd), full(ws), full(g), full(b)],
        out_specs=pl.BlockSpec((tm, d), lambda i: (i, 0)),
        out_shape=jax.ShapeDtypeStruct((m, d), F32),
        compiler_params=_params(("parallel",)),
        name="outproj_ln",
    )(og, od, os_, h, wg, wd, ws, g, b)


def _memkv_body(mem_ref, w_ref, kv_ref):
    kv_ref[...] = _dot(mem_ref[...].astype(BF16), w_ref[...]).astype(kv_ref.dtype)


def _memkv(mem2d, wkv, tm=512):
    m, d = mem2d.shape
    n = wkv.shape[1]
    return pl.pallas_call(
        _memkv_body,
        grid=(m // tm,),
        in_specs=[pl.BlockSpec((tm, d), lambda i: (i, 0)), pl.BlockSpec(wkv.shape, lambda i: (0, 0))],
        out_specs=pl.BlockSpec((tm, n), lambda i: (i, 0)),
        out_shape=jax.ShapeDtypeStruct((m, n), BF16),
        compiler_params=_params(("parallel",)),
        name="mem_kv",
    )(mem2d, wkv)


def _memattn_body(h_ref, kv_ref, wq_ref, wo_ref, g_ref, b_ref, o_ref, ctx_ref):
    d = h_ref.shape[1]
    dh = d // MEM_HEADS
    h = h_ref[...]
    q = _dot(h.astype(BF16), wq_ref[...]).astype(BF16)
    for hd in range(MEM_HEADS):
        cs = slice(hd * dh, (hd + 1) * dh)
        s = _dot_nt(q[:, cs], kv_ref[:, cs]) * (dh ** -0.5)
        s = s - jnp.max(s, axis=-1, keepdims=True)
        p = jnp.exp(s)
        p = p / jnp.sum(p, axis=-1, keepdims=True)
        ctx_ref[:, cs] = _dot(p.astype(BF16), kv_ref[:, d + hd * dh:d + (hd + 1) * dh]).astype(BF16)
    f = _dot(ctx_ref[...], wo_ref[...])
    o_ref[...] = _ln(ALPHA * h + f, g_ref[...], b_ref[...])


def _memattn(h, kv, wq, wo, g, b, batch, seq, n_mem, tm=512):
    m, d = h.shape
    nt = seq // tm
    full = lambda a: pl.BlockSpec(a.shape, lambda bb, t: (0, 0))
    return pl.pallas_call(
        _memattn_body,
        grid=(batch, nt),
        in_specs=[
            pl.BlockSpec((tm, d), lambda bb, t: (bb * nt + t, 0)),
            pl.BlockSpec((n_mem, 2 * d), lambda bb, t: (bb, 0)),
            full(wq), full(wo), full(g), full(b),
        ],
        out_specs=pl.BlockSpec((tm, d), lambda bb, t: (bb * nt + t, 0)),
        out_shape=jax.ShapeDtypeStruct((m, d), F32),
        scratch_shapes=[pltpu.VMEM((tm, d), BF16)],
        compiler_params=_params(("parallel", "parallel")),
        name="mem_attn_ln",
    )(h, kv, wq, wo, g, b)


MLP_FC = 512


def _mlp_body(h_ref, wu_ref, bu_ref, wd_ref, bd_ref, g_ref, b_ref, o_ref):
    h = h_ref[...]
    hb = h.astype(BF16)
    f = None
    for c0 in range(0, wu_ref.shape[1], MLP_FC):
        cs = slice(c0, c0 + MLP_FC)
        u = jnp.maximum(_dot(hb, wu_ref[:, cs]) + bu_ref[:, cs], 0.0)
        d = _dot((u * u).astype(BF16), wd_ref[cs, :])
        f = d if f is None else f + d
    o_ref[...] = _ln(ALPHA * h + (f + bd_ref[...]), g_ref[...], b_ref[...])


def _mlp(h, wu, bu, wd, bd, g, b, tm=512):
    m, d = h.shape
    const = lambda a: pl.BlockSpec(a.shape, lambda i: (0, 0), pipeline_mode=pl.Buffered(1))
    return pl.pallas_call(
        _mlp_body,
        grid=(m // tm,),
        in_specs=[pl.BlockSpec((tm, d), lambda i: (i, 0)),
                  const(wu), const(bu), const(wd), const(bd), const(g), const(b)],
        out_specs=pl.BlockSpec((tm, d), lambda i: (i, 0)),
        out_shape=jax.ShapeDtypeStruct((m, d), F32),
        compiler_params=_params(("parallel",)),
        name="mlp_ln",
    )(h, wu, bu, wd, bd, g, b)


def _pad_heads(w, heads, width, padded):
    lead = w.shape[:-1]
    w = w.reshape(lead + (heads, width))
    w = jnp.pad(w, [(0, 0)] * len(lead) + [(0, 0), (0, padded - width)])
    return w.reshape(lead + (heads * padded,))


def _split_w_in(w_in):
    sizes = (GLA_HEADS * GLA_DK, GLA_HEADS * GLA_DK, GLA_HEADS * GLA_DV, GLA_RANK, GLA_HEADS * GLA_DV,
             DSA_HEADS * DSA_DH, DSA_DH, DSA_LATENT, IDX_HEADS * IDX_DIM, IDX_DIM, IDX_HEADS,
             SB_HEADS * SB_DH, SB_HEADS * SB_DH, SB_HEADS * SB_DH)
    parts = []
    o = 0
    for s in sizes:
        parts.append(w_in[:, o:o + s])
        o += s
    gq, gk, gv, glr, gg, dq, dk, dv, iq, ik, iw, sq, sk, sv = parts
    w_gla = jnp.concatenate([
        _pad_heads(gq, GLA_HEADS, GLA_DK, GLA_DKP), _pad_heads(gk, GLA_HEADS, GLA_DK, GLA_DKP),
        _pad_heads(gg, GLA_HEADS, GLA_DV, GLA_DVP)], axis=1)
    w_gv = _pad_heads(gv, GLA_HEADS, GLA_DV, GLA_DVP)
    w_glr = jnp.pad(glr, ((0, 0), (0, LANE - GLA_RANK)))
    w_dkk = jnp.concatenate([ik, dk], axis=1)
    w_iw = jnp.pad(iw, ((0, 0), (0, LANE - IDX_HEADS)))
    return [w.astype(BF16) for w in (w_gla, w_gv, w_gv.T, w_glr, iq, dq, w_dkk, dv.T, w_iw, sq, sk, sv.T)]


_INPROJ_DTYPES = (F32, BF16, BF16, BF16, BF16, BF16, BF16, BF16, F32, BF16, BF16, BF16)
_INPROJ_TRANSPOSED = (False, False, True, False, False, False, False, True, False, False, False, True)


def kernel(x, mem, ln_in_g, ln_in_b, w_in, gla_gate_w2, gla_gate_b, gla_norm_g, dsa_w_uv, w_out,
           ln_mix_g, ln_mix_b, w_mem_q, w_mem_kv, w_mem_o, ln_mem_g, ln_mem_b,
           w_up, b_up, w_down, b_down, ln_ffn_g, ln_ffn_b):
    batch, seq, d = x.shape
    n_mem = mem.shape[1]
    depth = w_in.shape[0]
    x2 = x.reshape(batch * seq, d)
    mem2 = mem.reshape(batch * n_mem, d)
    row = lambda a: a.reshape(1, -1)
    gw, dw = GLA_HEADS * GLA_DV, DSA_HEADS * DSA_DH

    h = None
    for l in range(depth):
        ws = _split_w_in(w_in[l])
        if l == 0:
            outs = _inproj(x2, ws, _INPROJ_DTYPES, _INPROJ_TRANSPOSED, ln=(ln_in_g, ln_in_b))
            h = outs[-1]
            outs = outs[:-1]
        else:
            outs = _inproj(h, ws, _INPROJ_DTYPES, _INPROJ_TRANSPOSED)
        p_gla, p_gv, p_gvt, p_glr, p_iq, p_dq, p_dkk, p_dvt, p_iw, p_sq, p_sk, p_svt = outs

        w2p = jnp.pad(_pad_heads(gla_gate_w2[l], GLA_HEADS, GLA_DK, GLA_DKP),
                      ((0, LANE - GLA_RANK), (0, 0))).astype(BF16)
        gbp = row(_pad_heads(gla_gate_b[l], GLA_HEADS, GLA_DK, GLA_DKP))
        ngp = row(jnp.pad(gla_norm_g[l], (0, GLA_DVP - GLA_DV)))
        o_gla = _gla(p_gla, p_gv, p_gvt, p_glr, w2p, gbp, ngp, batch, seq)
        wuvt = jnp.swapaxes(dsa_w_uv[l], 1, 2).astype(BF16)
        o_dsa = _dsa(p_iq, p_dq, p_iw, p_dkk, p_dvt, wuvt, batch, seq)
        o_sb = _sb(p_sq, p_sk, p_svt, batch, seq)

        wo = w_out[l]
        wg = jnp.pad(wo[:gw].reshape(GLA_HEADS, GLA_DV, d), ((0, 0), (0, GLA_DVP - GLA_DV), (0, 0)))
        wg = wg.reshape(GLA_HEADS * GLA_DVP, d).astype(BF16)
        h = _outproj(o_gla, o_dsa, o_sb, h, wg, wo[gw:gw + dw].astype(BF16), wo[gw + dw:].astype(BF16),
                     row(ln_mix_g[l]), row(ln_mix_b[l]))

        kv = _memkv(mem2, w_mem_kv[l].astype(BF16))
        h = _memattn(h, kv, w_mem_q[l].astype(BF16), w_mem_o[l].astype(BF16),
                     row(ln_mem_g[l]), row(ln_mem_b[l]), batch, seq, n_mem)

        h = _mlp(h, w_up[l].astype(BF16), row(b_up[l]), w_down[l].astype(BF16), row(b_down[l]),
                 row(ln_ffn_g[l]), row(ln_ffn_b[l]))
    return h.reshape(batch, seq, d)
```

```python
import functools

import jax
import jax.numpy as jnp
import numpy as np
from jax import lax
from jax.experimental import pallas as pl
from jax.experimental.pallas import tpu as pltpu

F32 = jnp.float32
BF16 = jnp.bfloat16
HALF = jnp.bfloat16

DEPTH = 2
LN_EPS = 1e-5
GLA_HEADS, GLA_DK, GLA_DV, GLA_RANK, GLA_TAU, GLA_CHUNK = 4, 48, 96, 16, 16.0, 64
GLA_DKP, GLA_DVP = 64, 128
DSA_HEADS, DSA_DH, DSA_LATENT = 5, 64, 128
IDX_HEADS, IDX_DIM, DSA_TOPK_MAX = 8, 64, 256
SB_HEADS, SB_DH = 5, 64
MEM_HEADS = 4
ALPHA = (2.0 * DEPTH) ** 0.25
ALIBI_SLOPES = tuple(2.0 ** (-8.0 * (i + 1) / DSA_HEADS) for i in range(DSA_HEADS))
ALIBI_TERMS = 3
NEG_INF = float("-inf")
F32_LOWEST = float(jnp.finfo(jnp.float32).min)
INT_MIN = -(2 ** 31)

LANE = 128
SUBLANE = 8
PACKED_SUBLANE = 16
VMEM_LIMIT = 56 * 1024 * 1024


def _dot(a, b):
    return jnp.dot(a, b, preferred_element_type=F32)


def _dot_nt(a, b):
    return lax.dot_general(a, b, (((1,), (1,)), ((), ())), preferred_element_type=F32)


def _ln(x, g, b):
    mu = jnp.mean(x, axis=-1, keepdims=True)
    xc = x - mu
    var = jnp.mean(xc * xc, axis=-1, keepdims=True)
    return xc * lax.rsqrt(var + LN_EPS) * g + b


def _softplus(x):
    return jnp.maximum(x, 0.0) + jnp.log(1.0 + jnp.exp(-jnp.abs(x)))


def _split_dot_left(m, x, terms):
    out = None
    r = x
    for t in range(terms):
        xb = r.astype(BF16)
        d = _dot(m, xb)
        out = d if out is None else out + d
        if t + 1 < terms:
            r = r - xb.astype(F32)
    return out


def _bf16_terms(x, n):
    terms = []
    r = np.float32(x)
    for _ in range(n):
        t = np.float32(np.asarray(r).astype(jnp.bfloat16))
        terms.append(float(t))
        r = np.float32(r - t)
    return tuple(terms)


def _params(sem):
    return pltpu.CompilerParams(dimension_semantics=sem, vmem_limit_bytes=VMEM_LIMIT)


PAD_W = 384
INPROJ_CHUNK = 512
INPROJ_STREAMS = (
    ("gla", 2 * GLA_HEADS * GLA_DKP + GLA_HEADS * GLA_DVP, F32, True, False),
    ("gv", GLA_HEADS * GLA_DVP, BF16, True, True),
    ("glr", LANE, BF16, True, False),
    ("iq", IDX_HEADS * IDX_DIM, BF16, True, False),
    ("dkk", IDX_DIM + DSA_DH, BF16, True, False),
    ("iw", LANE, F32, True, False),
    ("dv", DSA_LATENT, BF16, False, True),
    ("dq", PAD_W, BF16, True, False),
    ("sq", PAD_W, BF16, True, False),
    ("sk", PAD_W, BF16, True, False),
    ("sv", PAD_W, BF16, False, True),
)
INPROJ_COLS = sum(s[1] for s in INPROJ_STREAMS)


def _inproj_outputs():
    outs = []
    for si, (_, _, _, plain, tr) in enumerate(INPROJ_STREAMS):
        if plain:
            outs.append((si, False))
        if tr:
            outs.append((si, True))
    return outs


def _inproj_body(*refs, apply_ln):
    if apply_ln:
        x_ref, g_ref, b_ref, w_ref = refs[:4]
        o_refs = refs[4:]
    else:
        x_ref, w_ref = refs[:2]
        o_refs = refs[2:]
    x = x_ref[...]
    if apply_ln:
        x = _ln(x, g_ref[...], b_ref[...])
        o_refs[-1][...] = x
    xb = x.astype(BF16)
    starts = np.cumsum([0] + [s[1] for s in INPROJ_STREAMS])
    outs = _inproj_outputs()
    for c0 in range(0, INPROJ_COLS, INPROJ_CHUNK):
        c1 = c0 + INPROJ_CHUNK
        y = _dot(xb, w_ref[:, c0:c1])
        for o_ref, (si, tr) in zip(o_refs, outs):
            a, b = max(int(starts[si]), c0), min(int(starts[si + 1]), c1)
            if a >= b:
                continue
            part = y[:, a - c0:b - c0]
            lo, hi = a - int(starts[si]), b - int(starts[si])
            if tr:
                o_ref[lo:hi, :] = part.T.astype(o_ref.dtype)
            else:
                o_ref[:, lo:hi] = part.astype(o_ref.dtype)


def _inproj(x, w_all, ln=None, tm=512):
    m, d = x.shape
    apply_ln = ln is not None
    in_specs = [pl.BlockSpec((tm, d), lambda i: (i, 0))]
    args = [x]
    if apply_ln:
        in_specs += [pl.BlockSpec((1, d), lambda i: (0, 0))] * 2
        args += [ln[0].reshape(1, d), ln[1].reshape(1, d)]
    in_specs.append(pl.BlockSpec(w_all.shape, lambda i: (0, 0)))
    args.append(w_all)
    out_shape, out_specs = [], []
    for si, tr in _inproj_outputs():
        _, width, dt, _, _ = INPROJ_STREAMS[si]
        if tr:
            out_shape.append(jax.ShapeDtypeStruct((width, m), dt))
            out_specs.append(pl.BlockSpec((width, tm), lambda i: (0, i)))
        else:
            out_shape.append(jax.ShapeDtypeStruct((m, width), dt))
            out_specs.append(pl.BlockSpec((tm, width), lambda i: (i, 0)))
    if apply_ln:
        out_shape.append(jax.ShapeDtypeStruct((m, d), F32))
        out_specs.append(pl.BlockSpec((tm, d), lambda i: (i, 0)))
    return pl.pallas_call(
        functools.partial(_inproj_body, apply_ln=apply_ln),
        grid=(m // tm,),
        in_specs=in_specs,
        out_specs=out_specs,
        out_shape=out_shape,
        compiler_params=_params(("parallel",)),
        name="inproj_ln" if apply_ln else "inproj",
    )(*args)


GLA_KW = GLA_HEADS * GLA_DKP
GLA_VW = GLA_HEADS * GLA_DVP
GLA_Q0, GLA_K0, GLA_G0 = 0, GLA_KW, 2 * GLA_KW
GLA_COLS = GLA_G0 + GLA_VW
GLA_TL = 256
GLA_NC = GLA_TL // GLA_CHUNK


def _gla_body(x_ref, v_ref, vt_ref, r_ref, w2_ref, gb_ref, ng_ref, o_ref, s_ref):
    @pl.when(pl.program_id(1) == 0)
    def _():
        s_ref[...] = jnp.zeros_like(s_ref)

    tl, c, nc = GLA_TL, GLA_CHUNK, GLA_NC
    row = lax.broadcasted_iota(jnp.int32, (tl, tl), 0)
    col = lax.broadcasted_iota(jnp.int32, (tl, tl), 1)
    rch = jnp.floor(row.astype(F32) * (1.0 / c))
    cch = jnp.floor(col.astype(F32) * (1.0 / c))
    causal = (rch == cch) & (col <= row)
    tri = causal.astype(BF16)
    heads = range(GLA_HEADS)
    ksl = [slice(h * GLA_DKP, (h + 1) * GLA_DKP) for h in heads]
    vsl = [slice(h * GLA_DVP, (h + 1) * GLA_DVP) for h in heads]

    xg = _dot(r_ref[...], w2_ref[...]) + gb_ref[...]
    log_a = -_softplus(-xg) / GLA_TAU
    bcum = _split_dot_left(tri, log_a, 3)
    krow = lax.broadcasted_iota(jnp.int32, (tl, GLA_KW), 0)
    blast = bcum[tl - 1:tl, :]
    for j in range(nc - 2, -1, -1):
        blast = jnp.where(krow < (j + 1) * c, bcum[(j + 1) * c - 1:(j + 1) * c, :], blast)
    q = x_ref[:, GLA_Q0:GLA_Q0 + GLA_KW]
    k = x_ref[:, GLA_K0:GLA_K0 + GLA_KW]
    q_t = (q * (GLA_DK ** -0.5) * jnp.exp(bcum)).astype(BF16)
    k_t = (k * jnp.exp(-bcum)).astype(BF16)
    k_end = (k * jnp.exp(blast - bcum)).astype(BF16)
    dec = [jnp.exp(bcum[(j + 1) * c - 1:(j + 1) * c, :]) for j in range(nc)]

    lane_chunk = jnp.floor(lax.broadcasted_iota(jnp.int32, (GLA_DVP, tl), 1).astype(F32) * (1.0 / c))
    o_intra, ds = [], []
    for h in heads:
        sc = jnp.where(causal, _dot_nt(q_t[:, ksl[h]], k_t[:, ksl[h]]), 0.0)
        o_intra.append(_dot(sc.astype(BF16), v_ref[:, vsl[h]]))
        vt = vt_ref[vsl[h], :]
        zero = jnp.zeros_like(vt)
        vt_by_chunk = jnp.concatenate([jnp.where(lane_chunk == float(j), vt, zero) for j in range(nc)], axis=0)
        ds.append(_dot(vt_by_chunk, k_end[:, ksl[h]]))

    ng = ng_ref[...]
    for h in heads:
        s = s_ref[h]
        states = []
        for j in range(nc):
            states.append(s.astype(BF16))
            s = s * dec[j][:, ksl[h]] + ds[h][j * GLA_DVP:(j + 1) * GLA_DVP, :]
        s_ref[h] = s
        inter_all = _dot_nt(q_t[:, ksl[h]], jnp.concatenate(states, axis=0))
        o_inter = jnp.concatenate(
            [inter_all[j * c:(j + 1) * c, j * GLA_DVP:(j + 1) * GLA_DVP] for j in range(nc)], axis=0)
        o = o_intra[h] + o_inter
        ms = jnp.sum(o * o, axis=-1, keepdims=True) * (1.0 / GLA_DV)
        on = o * lax.rsqrt(ms + LN_EPS) * ng
        g = x_ref[:, GLA_G0 + h * GLA_DVP:GLA_G0 + (h + 1) * GLA_DVP]
        o_ref[:, vsl[h]] = (on * (g / (1.0 + jnp.exp(-g)))).astype(o_ref.dtype)


def _gla(x32, v, vt, glr, w2p, gbp, ngp, batch, seq):
    tl = GLA_TL
    nt = seq // tl
    const = lambda a: pl.BlockSpec(a.shape, lambda b, t: (0, 0))
    return pl.pallas_call(
        _gla_body,
        grid=(batch, nt),
        in_specs=[
            pl.BlockSpec((tl, GLA_COLS), lambda b, t: (b * nt + t, 0)),
            pl.BlockSpec((tl, GLA_VW), lambda b, t: (b * nt + t, 0)),
            pl.BlockSpec((GLA_VW, tl), lambda b, t: (0, b * nt + t)),
            pl.BlockSpec((tl, LANE), lambda b, t: (b * nt + t, 0)),
            const(w2p), const(gbp), const(ngp),
        ],
        out_specs=pl.BlockSpec((tl, GLA_VW), lambda b, t: (b * nt + t, 0)),
        out_shape=jax.ShapeDtypeStruct((batch * seq, GLA_VW), BF16),
        scratch_shapes=[pltpu.VMEM((GLA_HEADS, GLA_DVP, GLA_DKP), F32)],
        compiler_params=_params(("arbitrary", "arbitrary")),
        name="gla",
    )(x32, v, vt, glr, w2p, gbp, ngp)


SB_TQ = 256
SB_CK = 256
SB_W = SB_HEADS * SB_DH
SB_WP = 384


def _sb_body(q_ref, k_ref, vt_ref, o_ref, qs_ref, acc_ref, run_ref):
    tq, ck = SB_TQ, SB_CK
    i = pl.program_id(1)
    c_last = (i + 1) * (tq // ck) - 1
    kloc = lax.broadcasted_iota(jnp.int32, (ck, tq), 0)
    qloc = lax.broadcasted_iota(jnp.int32, (ck, tq), 1)
    qpos = i * tq + qloc
    later = (lax.broadcasted_iota(jnp.int32, (ck, ck), 1) > lax.broadcasted_iota(jnp.int32, (ck, ck), 0)).astype(BF16)
    for h in range(SB_HEADS):
        qs_ref[h] = q_ref[:, h * SB_DH:(h + 1) * SB_DH] * (SB_DH ** -0.5)
    acc_ref[...] = jnp.zeros_like(acc_ref)
    run_ref[...] = jnp.zeros_like(run_ref)

    heads = range(SB_HEADS)
    hsl = [slice(h * SB_DH, (h + 1) * SB_DH) for h in heads]

    def chunk(c, masked):
        k0 = pl.multiple_of(c * ck, ck)
        zs = [_dot_nt(k_ref[pl.ds(k0, ck), hsl[h]], qs_ref[h]) for h in heads]
        runs = run_ref[...]
        sps = [_softplus(z) for z in zs]
        l1s = [-sp for sp in sps]
        if masked:
            valid = k0 + kloc < qpos
            l1s = [jnp.where(valid, l1, 0.0) for l1 in l1s]
        suf = _split_dot_left(later, jnp.concatenate(l1s, axis=1), 2)
        new_runs = []
        for h in heads:
            w = jnp.exp(zs[h] - sps[h] + suf[:, h * tq:(h + 1) * tq] + runs[h:h + 1, :])
            if masked:
                w = jnp.where(valid, w, 0.0)
            acc_ref[hsl[h], :] += _dot(vt_ref[hsl[h], pl.ds(k0, ck)], w.astype(BF16))
            new_runs.append(runs[h:h + 1, :] + jnp.sum(l1s[h], axis=0, keepdims=True))
        new_runs = jnp.concatenate(new_runs, axis=0)
        run_ref[0:SB_HEADS, :] = new_runs
        return (jnp.max(jnp.exp(new_runs + 2.0)) > 0.0).astype(jnp.int32)

    n_diag = tq // ck
    alive = jnp.int32(1)
    for j in range(n_diag):
        alive = chunk(c_last - j, True)

    def cond(st):
        step, alive = st
        return (step <= c_last) & (alive > 0)

    def body(st):
        step, _ = st
        return step + 1, chunk(c_last - step, False)

    lax.while_loop(cond, body, (jnp.int32(n_diag), alive))
    o_ref[...] = acc_ref[...].T[:, :SB_W].astype(o_ref.dtype)


def _sb(q, k, vt, batch, seq):
    tq = SB_TQ
    nq = seq // tq
    return pl.pallas_call(
        _sb_body,
        grid=(batch, nq),
        in_specs=[
            pl.BlockSpec((tq, q.shape[1]), lambda b, i: (b * nq + i, 0)),
            pl.BlockSpec((seq, k.shape[1]), lambda b, i: (b, 0)),
            pl.BlockSpec((vt.shape[0], seq), lambda b, i: (0, b)),
        ],
        out_specs=pl.BlockSpec((tq, SB_W), lambda b, i: (b * nq + i, 0)),
        out_shape=jax.ShapeDtypeStruct((batch * seq, SB_W), BF16),
        scratch_shapes=[
            pltpu.VMEM((SB_HEADS, tq, SB_DH), BF16),
            pltpu.VMEM((SB_WP, tq), F32),
            pltpu.VMEM((SUBLANE, tq), F32),
        ],
        compiler_params=_params(("parallel", "arbitrary")),
        name="sb",
    )(q, k, vt)


DSA_TQ = 256
DSA_CK = 256
DSA_NACC = 4
DSA_W = DSA_HEADS * DSA_DH
DSA_WP = 384
DKK_IK0, DKK_K0 = 0, IDX_DIM
DKK_COLS = IDX_DIM + DSA_DH


def _dsa_body(iq_ref, dq_ref, iw_ref, kk_ref, vt_ref, wuvt_ref, o_ref,
              isc_ref, ihi_ref, thr_ref, iqs_ref, iwt_ref, qs_ref, kaug_ref, sraw_ref,
              m_ref, l_ref, acc_ref, ot_ref,
              *, seq, topk):
    tq, ck = DSA_TQ, DSA_CK
    grp = ck // SUBLANE
    i = pl.program_id(1)
    nch = (i + 1) * (tq // ck)
    kloc = lax.broadcasted_iota(jnp.int32, (ck, tq), 0)
    qloc = lax.broadcasted_iota(jnp.int32, (ck, tq), 1)
    qpos = i * tq + qloc
    rowpos = i * tq + lax.broadcasted_iota(jnp.int32, (1, tq), 1)
    full = rowpos >= topk

    for h in range(IDX_HEADS):
        iqs_ref[h] = iq_ref[:, h * IDX_DIM:(h + 1) * IDX_DIM]
    qlane = lax.broadcasted_iota(jnp.int32, (tq, LANE - DSA_DH), 1)
    for h in range(DSA_HEADS):
        hr = slice(h * tq, (h + 1) * tq)
        qs_ref[hr, 0:DSA_DH] = dq_ref[:, h * DSA_DH:(h + 1) * DSA_DH] * (DSA_DH ** -0.5)
        pieces = jnp.zeros((tq, LANE - DSA_DH), F32)
        for t, piece in enumerate(_bf16_terms(ALIBI_SLOPES[h], ALIBI_TERMS)):
            pieces = jnp.where(qlane == t, piece, pieces)
        qs_ref[hr, DSA_DH:LANE] = pieces.astype(BF16)
    klane = lax.broadcasted_iota(jnp.int32, (ck, LANE - DSA_DH), 1)
    koff = lax.broadcasted_iota(jnp.int32, (ck, LANE - DSA_DH), 0).astype(F32)
    kaug_ref[:, DSA_DH:LANE] = jnp.where(klane < ALIBI_TERMS, koff, 0.0).astype(BF16)
    iwt_ref[...] = (iw_ref[...] * ((IDX_HEADS ** -0.5) * (IDX_DIM ** -0.5))).T[0:IDX_HEADS, :]

    def scores(c):
        k0 = pl.multiple_of(c * ck, ck)
        kaug_ref[:, 0:DSA_DH] = kk_ref[pl.ds(k0, ck), DKK_K0:DKK_K0 + DSA_DH]
        return _dot_nt(kaug_ref[...], qs_ref[...])

    sraw_ref[...] = scores(0)

    def index_chunk(c, carry):
        k0 = pl.multiple_of(c * ck, ck)
        ik = kk_ref[pl.ds(k0, ck), DKK_IK0:DKK_IK0 + IDX_DIM]
        acc = jnp.zeros((ck, tq), F32)
        for h in range(IDX_HEADS):
            rel = jnp.maximum(_dot_nt(ik, iqs_ref[h]), 0.0)
            acc = acc + rel * iwt_ref[h:h + 1, :]
        sc = jnp.where(k0 + kloc <= qpos, acc, NEG_INF)
        isc_ref[pl.ds(k0, ck), :] = sc
        ihi_ref[pl.ds(k0, ck), :] = sc.astype(HALF)
        return carry

    lax.fori_loop(0, nch, index_chunk, 0)

    thr_ref[...] = jnp.full(thr_ref.shape, F32_LOWEST, F32)

    def count(pred):
        def chunk(c, a):
            k0 = pl.multiple_of(c * ck, ck)
            blk = isc_ref[pl.ds(k0, ck), :].reshape(grp, SUBLANE, tq)
            hit = pred(blk, k0).astype(F32).reshape(grp // DSA_NACC, DSA_NACC, SUBLANE, tq)
            return a + jnp.sum(hit, axis=0)
        a = lax.fori_loop(0, nch, chunk, jnp.zeros((DSA_NACC, SUBLANE, tq), F32))
        return jnp.sum(a.reshape(DSA_NACC * SUBLANE, tq), axis=0, keepdims=True)

    def key_to_float(u):
        sk = u ^ INT_MIN
        return pltpu.bitcast(jnp.where(sk >= 0, sk, sk ^ 0x7FFFFFFF), F32)

    def rows8(v):
        return jnp.broadcast_to(v, (SUBLANE, tq))[None]

    def count_high(cand_hi):
        grp16 = ck // PACKED_SUBLANE
        cb = jnp.broadcast_to(cand_hi, (PACKED_SUBLANE, tq))[None]
        one = jnp.ones((), HALF)
        zero = jnp.zeros((), HALF)

        def chunk(c, a):
            k0 = pl.multiple_of(c * ck, ck)
            blk = ihi_ref[pl.ds(k0, ck), :].reshape(grp16, PACKED_SUBLANE, tq)
            hit = jnp.where(blk >= cb, one, zero).reshape(grp16 // DSA_NACC, DSA_NACC, PACKED_SUBLANE, tq)
            for g in range(grp16 // DSA_NACC):
                a = a + hit[g]
            return a
        a = lax.fori_loop(0, nch, chunk, jnp.zeros((DSA_NACC, PACKED_SUBLANE, tq), HALF))
        return jnp.sum(a.astype(F32).reshape(DSA_NACC * PACKED_SUBLANE, tq), axis=0, keepdims=True)

    @pl.when((i + 1) * tq > topk)
    def _search():
        def half_key(u):
            return u | jnp.where((u ^ INT_MIN) < 0, 0xFFFF, 0)

        def coarse_step(it, cur):
            cand = cur | lax.shift_left(jnp.int32(1), 31 - it)
            cnt = count_high(key_to_float(half_key(cand)).astype(HALF))
            return jnp.where(cnt >= topk, cand, cur)

        coarse = lax.fori_loop(0, 16, coarse_step, jnp.zeros((1, tq), jnp.int32))
        base = half_key(coarse) - 0x8000

        def count_ge(u):
            cf = rows8(key_to_float(u))
            return count(lambda blk, k0: blk >= cf)

        def fine_step(it, st):
            off, n_ge = st
            cand = off | lax.shift_left(jnp.int32(1), 16 - it)
            cnt = count_ge(base + cand)
            ok = cnt >= topk
            return jnp.where(ok, cand, off), jnp.where(ok, cnt, n_ge)

        off, n_ge = lax.fori_loop(0, 17, fine_step, (jnp.zeros((1, tq), jnp.int32), count_ge(base)))
        thr = jnp.where(full, key_to_float(base + off), F32_LOWEST)
        thr_ref[...] = thr
        excess = jnp.max(jnp.where(full, n_ge - topk, 0.0))

        @pl.when(excess > 0.0)
        def _ties():
            thr8 = rows8(thr)
            need = topk - count(lambda blk, k0: blk > thr8)
            nbits = (seq - 1).bit_length()
            kpos3 = lax.broadcasted_iota(jnp.int32, (grp, SUBLANE, tq), 0) * SUBLANE + \
                lax.broadcasted_iota(jnp.int32, (grp, SUBLANE, tq), 1)

            def pos_step(it, p):
                cand = p | lax.shift_left(jnp.int32(1), nbits - 1 - it)
                cand8 = rows8(cand)
                cnt = count(lambda blk, k0: (blk == thr8) & (k0 + kpos3 < cand8))
                return jnp.where(cnt < need, cand, p)

            last = lax.fori_loop(0, nbits, pos_step, jnp.zeros((1, tq), jnp.int32))

            def drop(c, carry):
                k0 = pl.multiple_of(c * ck, ck)
                blk = isc_ref[pl.ds(k0, ck), :]
                kill = (blk == thr) & (k0 + kloc > last) & full
                isc_ref[pl.ds(k0, ck), :] = jnp.where(kill, NEG_INF, blk)
                return carry

            lax.fori_loop(0, nch, drop, 0)

    m_ref[...] = jnp.full(m_ref.shape, NEG_INF, F32)
    l_ref[...] = jnp.zeros_like(l_ref)
    acc_ref[...] = jnp.zeros_like(acc_ref)
    thr = thr_ref[...]
    slope_row = jnp.concatenate([jnp.full((1, tq), sl, F32) for sl in ALIBI_SLOPES], axis=1)

    def attend(c, carry):
        k0 = pl.multiple_of(c * ck, ck)
        s5 = sraw_ref[...]
        s_next = scores(jnp.minimum(c + 1, nch - 1))
        sel = isc_ref[pl.ds(k0, ck), :] >= thr
        s5 = jnp.concatenate([jnp.where(sel, s5[:, h * tq:(h + 1) * tq], NEG_INF)
                              for h in range(DSA_HEADS)], axis=1)
        cvec = slope_row * k0.astype(F32)
        m_old = m_ref[...]
        m_new = jnp.maximum(m_old, jnp.max(s5, axis=0, keepdims=True) + cvec)
        m_safe = jnp.where(m_new == NEG_INF, 0.0, m_new)
        alpha = jnp.exp(m_old - m_safe)
        p = jnp.exp(s5 - (m_safe - cvec))
        l_ref[...] = alpha * l_ref[...] + jnp.sum(p, axis=0, keepdims=True)
        acc_ref[...] = alpha * acc_ref[...] + _dot(vt_ref[:, pl.ds(k0, ck)], p.astype(BF16))
        m_ref[...] = m_new
        sraw_ref[...] = s_next
        return carry

    lax.fori_loop(0, nch, attend, 0)

    ot_ref[DSA_W:, :] = jnp.zeros((DSA_WP - DSA_W, tq), F32)
    for h in range(DSA_HEADS):
        hs = slice(h * tq, (h + 1) * tq)
        o_lat = acc_ref[:, hs] / l_ref[:, hs]
        ot_ref[h * DSA_DH:(h + 1) * DSA_DH, :] = _dot(wuvt_ref[h], o_lat.astype(BF16))
    o_ref[...] = ot_ref[...].T[:, :DSA_W].astype(o_ref.dtype)


def _dsa(iq, dq, iw, dkk, dvt, wuvt, batch, seq):
    tq = DSA_TQ
    nq = seq // tq
    topk = min(DSA_TOPK_MAX, seq // 4)
    return pl.pallas_call(
        functools.partial(_dsa_body, seq=seq, topk=topk),
        grid=(batch, nq),
        in_specs=[
            pl.BlockSpec((tq, iq.shape[1]), lambda b, i: (b * nq + i, 0)),
            pl.BlockSpec((tq, dq.shape[1]), lambda b, i: (b * nq + i, 0)),
            pl.BlockSpec((tq, iw.shape[1]), lambda b, i: (b * nq + i, 0)),
            pl.BlockSpec((seq, DKK_COLS), lambda b, i: (b, 0)),
            pl.BlockSpec((DSA_LATENT, seq), lambda b, i: (0, b)),
            pl.BlockSpec(wuvt.shape, lambda b, i: (0, 0, 0)),
        ],
        out_specs=pl.BlockSpec((tq, DSA_W), lambda b, i: (b * nq + i, 0)),
        out_shape=jax.ShapeDtypeStruct((batch * seq, DSA_W), BF16),
        scratch_shapes=[
            pltpu.VMEM((seq, tq), F32),
            pltpu.VMEM((seq, tq), HALF),
            pltpu.VMEM((1, tq), F32),
            pltpu.VMEM((IDX_HEADS, tq, IDX_DIM), BF16),
            pltpu.VMEM((IDX_HEADS, tq), F32),
            pltpu.VMEM((DSA_HEADS * tq, LANE), BF16),
            pltpu.VMEM((DSA_CK, LANE), BF16),
            pltpu.VMEM((DSA_CK, DSA_HEADS * tq), F32),
            pltpu.VMEM((1, DSA_HEADS * tq), F32),
            pltpu.VMEM((1, DSA_HEADS * tq), F32),
            pltpu.VMEM((DSA_LATENT, DSA_HEADS * tq), F32),
            pltpu.VMEM((DSA_WP, tq), F32),
        ],
        compiler_params=_params(("parallel", "arbitrary")),
        name="dsa",
    )(iq, dq, iw, dkk, dvt, wuvt)


def _outproj_body(og_ref, od_ref, os_ref, h_ref, wg_ref, wd_ref, ws_ref, g_ref, b_ref, o_ref):
    f = _dot(og_ref[...], wg_ref[...]) + _dot(od_ref[...], wd_ref[...]) + _dot(os_ref[...], ws_ref[...])
    o_ref[...] = _ln(ALPHA * h_ref[...] + f, g_ref[...], b_ref[...])


def _memkv_body(mem_ref, w_ref, kv_ref):
    kv_ref[...] = _dot(mem_ref[...].astype(BF16), w_ref[...]).astype(kv_ref.dtype)


def _memkv(mem2d, wkv, tm=512):
    m, d = mem2d.shape
    n = wkv.shape[1]
    return pl.pallas_call(
        _memkv_body,
        grid=(m // tm,),
        in_specs=[pl.BlockSpec((tm, d), lambda i: (i, 0)), pl.BlockSpec(wkv.shape, lambda i: (0, 0))],
        out_specs=pl.BlockSpec((tm, n), lambda i: (i, 0)),
        out_shape=jax.ShapeDtypeStruct((m, n), BF16),
        compiler_params=_params(("parallel",)),
        name="mem_kv",
    )(mem2d, wkv)


def _memattn_body(h_ref, kv_ref, wq_ref, wo_ref, g_ref, b_ref, o_ref, ctx_ref):
    tm, d = h_ref.shape
    dh = d // MEM_HEADS
    groups = [slice(r0, r0 + tm // 2) for r0 in (0, tm // 2)]
    qs = [_dot(h_ref[rs, :].astype(BF16), wq_ref[...]).astype(BF16) for rs in groups]

    def attend(rs, q):
        def scores(hd):
            cs = slice(hd * dh, (hd + 1) * dh)
            return _dot_nt(q[:, cs], kv_ref[:, cs]) * (dh ** -0.5)

        s = scores(0)
        ctx_prev = None
        for hd in range(MEM_HEADS):
            s_next = scores(hd + 1) if hd + 1 < MEM_HEADS else None
            s = s - jnp.max(s, axis=-1, keepdims=True)
            p = jnp.exp(s)
            p = p / jnp.sum(p, axis=-1, keepdims=True)
            ctx = _dot(p.astype(BF16), kv_ref[:, d + hd * dh:d + (hd + 1) * dh])
            if ctx_prev is not None:
                ctx_ref[rs, (hd - 1) * dh:hd * dh] = ctx_prev.astype(BF16)
            ctx_prev = ctx
            s = s_next
        ctx_ref[rs, (MEM_HEADS - 1) * dh:] = ctx_prev.astype(BF16)
        return _dot(ctx_ref[rs, :], wo_ref[...])

    fs = [attend(rs, q) for rs, q in zip(groups, qs)]
    for rs, f in zip(groups, fs):
        o_ref[rs, :] = _ln(ALPHA * h_ref[rs, :] + f, g_ref[...], b_ref[...])


MLP_FC = 512


def _mlp_body(h_ref, wu_ref, bu_ref, wd_ref, bd_ref, g_ref, b_ref, o_ref):
    h = h_ref[...]
    hb = h.astype(BF16)
    f = None
    for c0 in range(0, wu_ref.shape[1], MLP_FC):
        cs = slice(c0, c0 + MLP_FC)
        u = jnp.maximum(_dot(hb, wu_ref[:, cs]) + bu_ref[:, cs], 0.0)
        d = _dot((u * u).astype(BF16), wd_ref[cs, :])
        f = d if f is None else f + d
    o_ref[...] = _ln(ALPHA * h + (f + bd_ref[...]), g_ref[...], b_ref[...])


def _tail_body(og_ref, od_ref, os_ref, h_ref, kv_ref, wg_ref, wd_ref, ws_ref, g1_ref, b1_ref,
               wq_ref, wo_ref, g2_ref, b2_ref, wu_ref, bu_ref, wdn_ref, bdn_ref, g3_ref, b3_ref,
               o_ref, hs_ref, ctx_ref):
    _outproj_body(og_ref, od_ref, os_ref, h_ref, wg_ref, wd_ref, ws_ref, g1_ref, b1_ref, hs_ref)
    _memattn_body(hs_ref, kv_ref, wq_ref, wo_ref, g2_ref, b2_ref, hs_ref, ctx_ref)
    _mlp_body(hs_ref, wu_ref, bu_ref, wdn_ref, bdn_ref, g3_ref, b3_ref, o_ref)


def _tail(og, od, os_, h, kv, consts, batch, seq, n_mem, tm=512):
    m, d = h.shape
    nt = seq // tm
    row = lambda a: pl.BlockSpec((tm, a.shape[1]), lambda bb, t: (bb * nt + t, 0))
    const = lambda a: pl.BlockSpec(a.shape, lambda bb, t: (0, 0), pipeline_mode=pl.Buffered(1))
    wg, wd, ws, g1, b1, wq, wo, g2, b2, wu, bu, wdn, bdn, g3, b3 = consts
    return pl.pallas_call(
        _tail_body,
        grid=(batch, nt),
        in_specs=[row(og), row(od), row(os_), row(h),
                  pl.BlockSpec((n_mem, 2 * d), lambda bb, t: (bb, 0))] + [const(a) for a in consts],
        out_specs=pl.BlockSpec((tm, d), lambda bb, t: (bb * nt + t, 0)),
        out_shape=jax.ShapeDtypeStruct((m, d), F32),
        scratch_shapes=[pltpu.VMEM((tm, d), F32), pltpu.VMEM((tm, d), BF16)],
        compiler_params=_params(("parallel", "parallel")),
        name="tail",
    )(og, od, os_, h, kv, *consts)


def _pad_heads(w, heads, width, padded):
    lead = w.shape[:-1]
    w = w.reshape(lead + (heads, width))
    w = jnp.pad(w, [(0, 0)] * len(lead) + [(0, 0), (0, padded - width)])
    return w.reshape(lead + (heads * padded,))


def _split_w_in(w_in):
    sizes = (GLA_HEADS * GLA_DK, GLA_HEADS * GLA_DK, GLA_HEADS * GLA_DV, GLA_RANK, GLA_HEADS * GLA_DV,
             DSA_HEADS * DSA_DH, DSA_DH, DSA_LATENT, IDX_HEADS * IDX_DIM, IDX_DIM, IDX_HEADS,
             SB_HEADS * SB_DH, SB_HEADS * SB_DH, SB_HEADS * SB_DH)
    parts = []
    o = 0
    for s in sizes:
        parts.append(w_in[:, o:o + s])
        o += s
    gq, gk, gv, glr, gg, dq, dk, dv, iq, ik, iw, sq, sk, sv = parts
    pad_to = lambda w, n: jnp.pad(w, ((0, 0), (0, n - w.shape[1])))
    pieces = {
        "gla": jnp.concatenate([
            _pad_heads(gq, GLA_HEADS, GLA_DK, GLA_DKP), _pad_heads(gk, GLA_HEADS, GLA_DK, GLA_DKP),
            _pad_heads(gg, GLA_HEADS, GLA_DV, GLA_DVP)], axis=1),
        "gv": _pad_heads(gv, GLA_HEADS, GLA_DV, GLA_DVP),
        "glr": pad_to(glr, LANE),
        "iq": iq,
        "dkk": jnp.concatenate([ik, dk], axis=1),
        "iw": pad_to(iw, LANE),
        "dv": dv,
        "dq": pad_to(dq, PAD_W),
        "sq": pad_to(sq, PAD_W),
        "sk": pad_to(sk, PAD_W),
        "sv": pad_to(sv, PAD_W),
    }
    for name, width, _, _, _ in INPROJ_STREAMS:
        assert pieces[name].shape[1] == width, name
    return jnp.concatenate([pieces[s[0]] for s in INPROJ_STREAMS], axis=1).astype(BF16)


def kernel(x, mem, ln_in_g, ln_in_b, w_in, gla_gate_w2, gla_gate_b, gla_norm_g, dsa_w_uv, w_out,
           ln_mix_g, ln_mix_b, w_mem_q, w_mem_kv, w_mem_o, ln_mem_g, ln_mem_b,
           w_up, b_up, w_down, b_down, ln_ffn_g, ln_ffn_b):
    batch, seq, d = x.shape
    n_mem = mem.shape[1]
    depth = w_in.shape[0]
    x2 = x.reshape(batch * seq, d)
    mem2 = mem.reshape(batch * n_mem, d)
    row = lambda a: a.reshape(1, -1)
    gw, dw = GLA_HEADS * GLA_DV, DSA_HEADS * DSA_DH

    h = None
    for l in range(depth):
        w_all = _split_w_in(w_in[l])
        if l == 0:
            outs = _inproj(x2, w_all, ln=(ln_in_g, ln_in_b))
            h = outs[-1]
            outs = outs[:-1]
        else:
            outs = _inproj(h, w_all)
        p_gla, p_gv, p_gvt, p_glr, p_iq, p_dkk, p_iw, p_dvt, p_dq, p_sq, p_sk, p_svt = outs

        w2p = jnp.pad(_pad_heads(gla_gate_w2[l], GLA_HEADS, GLA_DK, GLA_DKP),
                      ((0, LANE - GLA_RANK), (0, 0))).astype(BF16)
        gbp = row(_pad_heads(gla_gate_b[l], GLA_HEADS, GLA_DK, GLA_DKP))
        ngp = row(jnp.pad(gla_norm_g[l], (0, GLA_DVP - GLA_DV)))
        o_gla = _gla(p_gla, p_gv, p_gvt, p_glr, w2p, gbp, ngp, batch, seq)
        wuvt = jnp.swapaxes(dsa_w_uv[l], 1, 2).astype(BF16)
        o_dsa = _dsa(p_iq, p_dq, p_iw, p_dkk, p_dvt, wuvt, batch, seq)
        o_sb = _sb(p_sq, p_sk, p_svt, batch, seq)

        wo = w_out[l]
        wg = jnp.pad(wo[:gw].reshape(GLA_HEADS, GLA_DV, d), ((0, 0), (0, GLA_DVP - GLA_DV), (0, 0)))
        wg = wg.reshape(GLA_HEADS * GLA_DVP, d).astype(BF16)
        kv = _memkv(mem2, w_mem_kv[l].astype(BF16))
        consts = (wg, wo[gw:gw + dw].astype(BF16), wo[gw + dw:].astype(BF16), row(ln_mix_g[l]), row(ln_mix_b[l]),
                  w_mem_q[l].astype(BF16), w_mem_o[l].astype(BF16), row(ln_mem_g[l]), row(ln_mem_b[l]),
                  w_up[l].astype(BF16), row(b_up[l]), w_down[l].astype(BF16), row(b_down[l]),
                  row(ln_ffn_g[l]), row(ln_ffn_b[l]))
        h = _tail(o_gla, o_dsa, o_sb, h, kv, consts, batch, seq, n_mem)
    return h.reshape(batch, seq, d)
```

```python
import functools

import jax
import jax.numpy as jnp
import numpy as np
from jax import lax
from jax.experimental import pallas as pl
from jax.experimental.pallas import tpu as pltpu

F32 = jnp.float32
BF16 = jnp.bfloat16
HALF = jnp.bfloat16

DEPTH = 2
LN_EPS = 1e-5
GLA_HEADS, GLA_DK, GLA_DV, GLA_RANK, GLA_TAU, GLA_CHUNK = 4, 48, 96, 16, 16.0, 64
GLA_DKP, GLA_DVP = 64, 128
DSA_HEADS, DSA_DH, DSA_LATENT = 5, 64, 128
IDX_HEADS, IDX_DIM, DSA_TOPK_MAX = 8, 64, 256
SB_HEADS, SB_DH = 5, 64
MEM_HEADS = 4
ALPHA = (2.0 * DEPTH) ** 0.25
ALIBI_SLOPES = tuple(2.0 ** (-8.0 * (i + 1) / DSA_HEADS) for i in range(DSA_HEADS))
ALIBI_TERMS = 3
NEG_INF = float("-inf")
F32_LOWEST = float(jnp.finfo(jnp.float32).min)
INT_MIN = -(2 ** 31)

LANE = 128
SUBLANE = 8
PACKED_SUBLANE = 16
VMEM_LIMIT = 56 * 1024 * 1024


def _dot(a, b):
    return jnp.dot(a, b, preferred_element_type=F32)


def _dot_nt(a, b):
    return lax.dot_general(a, b, (((1,), (1,)), ((), ())), preferred_element_type=F32)


def _ln(x, g, b):
    mu = jnp.mean(x, axis=-1, keepdims=True)
    xc = x - mu
    var = jnp.mean(xc * xc, axis=-1, keepdims=True)
    return xc * lax.rsqrt(var + LN_EPS) * g + b


def _softplus(x):
    return jnp.maximum(x, 0.0) + jnp.log(1.0 + jnp.exp(-jnp.abs(x)))


def _split_dot_left(m, x, terms):
    out = None
    r = x
    for t in range(terms):
        xb = r.astype(BF16)
        d = _dot(m, xb)
        out = d if out is None else out + d
        if t + 1 < terms:
            r = r - xb.astype(F32)
    return out


def _bf16_terms(x, n):
    terms = []
    r = np.float32(x)
    for _ in range(n):
        t = np.float32(np.asarray(r).astype(jnp.bfloat16))
        terms.append(float(t))
        r = np.float32(r - t)
    return tuple(terms)


def _layer_block(a, layer, buffered=False):
    zeros = (0,) * (a.ndim - 1)
    kw = dict(pipeline_mode=pl.Buffered(1)) if buffered else {}
    return pl.BlockSpec((None,) + a.shape[1:], lambda *_: (layer,) + zeros, **kw)


def _params(sem):
    return pltpu.CompilerParams(dimension_semantics=sem, vmem_limit_bytes=VMEM_LIMIT)


PAD_W = 384
INPROJ_CHUNK = 512
INPROJ_STREAMS = (
    ("gla", 2 * GLA_HEADS * GLA_DKP + GLA_HEADS * GLA_DVP, F32, True, False),
    ("gv", GLA_HEADS * GLA_DVP, BF16, True, True),
    ("glr", LANE, BF16, True, False),
    ("iq", IDX_HEADS * IDX_DIM, BF16, True, False),
    ("dkk", IDX_DIM + DSA_DH, BF16, True, False),
    ("iw", LANE, F32, True, False),
    ("dv", DSA_LATENT, BF16, False, True),
    ("dq", PAD_W, BF16, True, False),
    ("sq", PAD_W, BF16, True, False),
    ("sk", PAD_W, BF16, True, False),
    ("sv", PAD_W, BF16, False, True),
)
INPROJ_COLS = sum(s[1] for s in INPROJ_STREAMS)


def _inproj_outputs():
    outs = []
    for si, (_, _, _, plain, tr) in enumerate(INPROJ_STREAMS):
        if plain:
            outs.append((si, False))
        if tr:
            outs.append((si, True))
    return outs


def _inproj_body(*refs, apply_ln):
    if apply_ln:
        x_ref, g_ref, b_ref, w_ref = refs[:4]
        o_refs = refs[4:]
    else:
        x_ref, w_ref = refs[:2]
        o_refs = refs[2:]
    x = x_ref[...]
    if apply_ln:
        x = _ln(x, g_ref[...], b_ref[...])
        o_refs[-1][...] = x
    xb = x.astype(BF16)
    starts = np.cumsum([0] + [s[1] for s in INPROJ_STREAMS])
    outs = _inproj_outputs()
    for c0 in range(0, INPROJ_COLS, INPROJ_CHUNK):
        c1 = c0 + INPROJ_CHUNK
        y = _dot(xb, w_ref[:, c0:c1])
        for o_ref, (si, tr) in zip(o_refs, outs):
            a, b = max(int(starts[si]), c0), min(int(starts[si + 1]), c1)
            if a >= b:
                continue
            part = y[:, a - c0:b - c0]
            lo, hi = a - int(starts[si]), b - int(starts[si])
            if tr:
                o_ref[lo:hi, :] = part.T.astype(o_ref.dtype)
            else:
                o_ref[:, lo:hi] = part.astype(o_ref.dtype)


def _inproj(x, w_all, layer, ln=None, tm=512):
    m, d = x.shape
    apply_ln = ln is not None
    in_specs = [pl.BlockSpec((tm, d), lambda i: (i, 0))]
    args = [x]
    if apply_ln:
        in_specs += [pl.BlockSpec((1, d), lambda i: (0, 0))] * 2
        args += [ln[0].reshape(1, d), ln[1].reshape(1, d)]
    in_specs.append(_layer_block(w_all, layer))
    args.append(w_all)
    out_shape, out_specs = [], []
    for si, tr in _inproj_outputs():
        _, width, dt, _, _ = INPROJ_STREAMS[si]
        if tr:
            out_shape.append(jax.ShapeDtypeStruct((width, m), dt))
            out_specs.append(pl.BlockSpec((width, tm), lambda i: (0, i)))
        else:
            out_shape.append(jax.ShapeDtypeStruct((m, width), dt))
            out_specs.append(pl.BlockSpec((tm, width), lambda i: (i, 0)))
    if apply_ln:
        out_shape.append(jax.ShapeDtypeStruct((m, d), F32))
        out_specs.append(pl.BlockSpec((tm, d), lambda i: (i, 0)))
    return pl.pallas_call(
        functools.partial(_inproj_body, apply_ln=apply_ln),
        grid=(m // tm,),
        in_specs=in_specs,
        out_specs=out_specs,
        out_shape=out_shape,
        compiler_params=_params(("parallel",)),
        name="inproj_ln" if apply_ln else "inproj",
    )(*args)


GLA_KW = GLA_HEADS * GLA_DKP
GLA_VW = GLA_HEADS * GLA_DVP
GLA_Q0, GLA_K0, GLA_G0 = 0, GLA_KW, 2 * GLA_KW
GLA_COLS = GLA_G0 + GLA_VW
GLA_TL = 256
GLA_NC = GLA_TL // GLA_CHUNK


def _gla_body(x_ref, v_ref, vt_ref, r_ref, w2_ref, gb_ref, ng_ref, o_ref, s_ref):
    @pl.when(pl.program_id(1) == 0)
    def _():
        s_ref[...] = jnp.zeros_like(s_ref)

    tl, c, nc = GLA_TL, GLA_CHUNK, GLA_NC
    row = lax.broadcasted_iota(jnp.int32, (tl, tl), 0)
    col = lax.broadcasted_iota(jnp.int32, (tl, tl), 1)
    rch = jnp.floor(row.astype(F32) * (1.0 / c))
    cch = jnp.floor(col.astype(F32) * (1.0 / c))
    causal = (rch == cch) & (col <= row)
    tri = causal.astype(BF16)
    heads = range(GLA_HEADS)
    ksl = [slice(h * GLA_DKP, (h + 1) * GLA_DKP) for h in heads]
    vsl = [slice(h * GLA_DVP, (h + 1) * GLA_DVP) for h in heads]

    xg = _dot(r_ref[...], w2_ref[...]) + gb_ref[...]
    log_a = -_softplus(-xg) / GLA_TAU
    bcum = _split_dot_left(tri, log_a, 3)
    krow = lax.broadcasted_iota(jnp.int32, (tl, GLA_KW), 0)
    blast = bcum[tl - 1:tl, :]
    for j in range(nc - 2, -1, -1):
        blast = jnp.where(krow < (j + 1) * c, bcum[(j + 1) * c - 1:(j + 1) * c, :], blast)
    q = x_ref[:, GLA_Q0:GLA_Q0 + GLA_KW]
    k = x_ref[:, GLA_K0:GLA_K0 + GLA_KW]
    q_t = (q * (GLA_DK ** -0.5) * jnp.exp(bcum)).astype(BF16)
    k_t = (k * jnp.exp(-bcum)).astype(BF16)
    k_end = (k * jnp.exp(blast - bcum)).astype(BF16)
    dec = [jnp.exp(bcum[(j + 1) * c - 1:(j + 1) * c, :]) for j in range(nc)]

    lane_chunk = jnp.floor(lax.broadcasted_iota(jnp.int32, (GLA_DVP, tl), 1).astype(F32) * (1.0 / c))
    o_intra, ds = [], []
    for h in heads:
        sc = jnp.where(causal, _dot_nt(q_t[:, ksl[h]], k_t[:, ksl[h]]), 0.0)
        o_intra.append(_dot(sc.astype(BF16), v_ref[:, vsl[h]]))
        vt = vt_ref[vsl[h], :]
        zero = jnp.zeros_like(vt)
        vt_by_chunk = jnp.concatenate([jnp.where(lane_chunk == float(j), vt, zero) for j in range(nc)], axis=0)
        ds.append(_dot(vt_by_chunk, k_end[:, ksl[h]]))

    ng = ng_ref[...]
    for h in heads:
        s = s_ref[h]
        states = []
        for j in range(nc):
            states.append(s.astype(BF16))
            s = s * dec[j][:, ksl[h]] + ds[h][j * GLA_DVP:(j + 1) * GLA_DVP, :]
        s_ref[h] = s
        inter_all = _dot_nt(q_t[:, ksl[h]], jnp.concatenate(states, axis=0))
        o_inter = jnp.concatenate(
            [inter_all[j * c:(j + 1) * c, j * GLA_DVP:(j + 1) * GLA_DVP] for j in range(nc)], axis=0)
        o = o_intra[h] + o_inter
        ms = jnp.sum(o * o, axis=-1, keepdims=True) * (1.0 / GLA_DV)
        on = o * lax.rsqrt(ms + LN_EPS) * ng
        g = x_ref[:, GLA_G0 + h * GLA_DVP:GLA_G0 + (h + 1) * GLA_DVP]
        o_ref[:, vsl[h]] = (on * (g / (1.0 + jnp.exp(-g)))).astype(o_ref.dtype)


def _gla(x32, v, vt, glr, w2p, gbp, ngp, layer, batch, seq):
    tl = GLA_TL
    nt = seq // tl
    const = lambda a: _layer_block(a, layer)
    return pl.pallas_call(
        _gla_body,
        grid=(batch, nt),
        in_specs=[
            pl.BlockSpec((tl, GLA_COLS), lambda b, t: (b * nt + t, 0)),
            pl.BlockSpec((tl, GLA_VW), lambda b, t: (b * nt + t, 0)),
            pl.BlockSpec((GLA_VW, tl), lambda b, t: (0, b * nt + t)),
            pl.BlockSpec((tl, LANE), lambda b, t: (b * nt + t, 0)),
            const(w2p), const(gbp), const(ngp),
        ],
        out_specs=pl.BlockSpec((tl, GLA_VW), lambda b, t: (b * nt + t, 0)),
        out_shape=jax.ShapeDtypeStruct((batch * seq, GLA_VW), BF16),
        scratch_shapes=[pltpu.VMEM((GLA_HEADS, GLA_DVP, GLA_DKP), F32)],
        compiler_params=_params(("arbitrary", "arbitrary")),
        name="gla",
    )(x32, v, vt, glr, w2p, gbp, ngp)


SB_TQ = 256
SB_CK = 256
SB_W = SB_HEADS * SB_DH
SB_WP = 384


def _sb_body(q_ref, k_ref, vt_ref, o_ref, qs_ref, acc_ref, run_ref):
    tq, ck = SB_TQ, SB_CK
    i = pl.program_id(1)
    c_last = (i + 1) * (tq // ck) - 1
    kloc = lax.broadcasted_iota(jnp.int32, (ck, tq), 0)
    qloc = lax.broadcasted_iota(jnp.int32, (ck, tq), 1)
    qpos = i * tq + qloc
    later = (lax.broadcasted_iota(jnp.int32, (ck, ck), 1) > lax.broadcasted_iota(jnp.int32, (ck, ck), 0)).astype(BF16)
    for h in range(SB_HEADS):
        qs_ref[h] = q_ref[:, h * SB_DH:(h + 1) * SB_DH] * (SB_DH ** -0.5)
    acc_ref[...] = jnp.zeros_like(acc_ref)
    run_ref[...] = jnp.zeros_like(run_ref)

    heads = range(SB_HEADS)
    hsl = [slice(h * SB_DH, (h + 1) * SB_DH) for h in heads]

    def chunk(c, masked):
        k0 = pl.multiple_of(c * ck, ck)
        zs = [_dot_nt(k_ref[pl.ds(k0, ck), hsl[h]], qs_ref[h]) for h in heads]
        runs = run_ref[...]
        sps = [_softplus(z) for z in zs]
        l1s = [-sp for sp in sps]
        if masked:
            valid = k0 + kloc < qpos
            l1s = [jnp.where(valid, l1, 0.0) for l1 in l1s]
        suf = _split_dot_left(later, jnp.concatenate(l1s, axis=1), 2)
        new_runs = []
        for h in heads:
            w = jnp.exp(zs[h] - sps[h] + suf[:, h * tq:(h + 1) * tq] + runs[h:h + 1, :])
            if masked:
                w = jnp.where(valid, w, 0.0)
            acc_ref[hsl[h], :] += _dot(vt_ref[hsl[h], pl.ds(k0, ck)], w.astype(BF16))
            new_runs.append(runs[h:h + 1, :] + jnp.sum(l1s[h], axis=0, keepdims=True))
        new_runs = jnp.concatenate(new_runs, axis=0)
        run_ref[0:SB_HEADS, :] = new_runs
        return (jnp.max(jnp.exp(new_runs + 2.0)) > 0.0).astype(jnp.int32)

    n_diag = tq // ck
    alive = jnp.int32(1)
    for j in range(n_diag):
        alive = chunk(c_last - j, True)

    def cond(st):
        step, alive = st
        return (step <= c_last) & (alive > 0)

    def body(st):
        step, _ = st
        return step + 1, chunk(c_last - step, False)

    lax.while_loop(cond, body, (jnp.int32(n_diag), alive))
    o_ref[...] = acc_ref[...].T[:, :SB_W].astype(o_ref.dtype)


def _sb(q, k, vt, batch, seq):
    tq = SB_TQ
    nq = seq // tq
    return pl.pallas_call(
        _sb_body,
        grid=(batch, nq),
        in_specs=[
            pl.BlockSpec((tq, q.shape[1]), lambda b, i: (b * nq + i, 0)),
            pl.BlockSpec((seq, k.shape[1]), lambda b, i: (b, 0)),
            pl.BlockSpec((vt.shape[0], seq), lambda b, i: (0, b)),
        ],
        out_specs=pl.BlockSpec((tq, SB_W), lambda b, i: (b * nq + i, 0)),
        out_shape=jax.ShapeDtypeStruct((batch * seq, SB_W), BF16),
        scratch_shapes=[
            pltpu.VMEM((SB_HEADS, tq, SB_DH), BF16),
            pltpu.VMEM((SB_WP, tq), F32),
            pltpu.VMEM((SUBLANE, tq), F32),
        ],
        compiler_params=_params(("parallel", "arbitrary")),
        name="sb",
    )(q, k, vt)


DSA_TQ = 256
DSA_CK = 256
DSA_NACC = 4
DSA_CGRP = 8
DSA_W = DSA_HEADS * DSA_DH
DSA_WP = 384
DKK_IK0, DKK_K0 = 0, IDX_DIM
DKK_COLS = IDX_DIM + DSA_DH


def _dsa_body(iq_ref, dq_ref, iw_ref, kk_ref, vt_ref, wuvt_ref, o_ref,
              isc_ref, ihi_ref, cand_ref, thr_ref, nge_ref, iqs_ref, iwt_ref, qs_ref, kaug_ref, sraw_ref,
              m_ref, l_ref, acc_ref, ot_ref,
              *, seq, topk):
    tq, ck = DSA_TQ, DSA_CK
    grp = ck // SUBLANE
    i = pl.program_id(1)
    nch = (i + 1) * (tq // ck)
    kloc = lax.broadcasted_iota(jnp.int32, (ck, tq), 0)
    qloc = lax.broadcasted_iota(jnp.int32, (ck, tq), 1)
    qpos = i * tq + qloc
    rowpos = i * tq + lax.broadcasted_iota(jnp.int32, (1, tq), 1)
    full = rowpos >= topk

    for h in range(IDX_HEADS):
        iqs_ref[h] = iq_ref[:, h * IDX_DIM:(h + 1) * IDX_DIM]
    qlane = lax.broadcasted_iota(jnp.int32, (tq, LANE - DSA_DH), 1)
    for h in range(DSA_HEADS):
        hr = slice(h * tq, (h + 1) * tq)
        qs_ref[hr, 0:DSA_DH] = dq_ref[:, h * DSA_DH:(h + 1) * DSA_DH] * (DSA_DH ** -0.5)
        pieces = jnp.zeros((tq, LANE - DSA_DH), F32)
        for t, piece in enumerate(_bf16_terms(ALIBI_SLOPES[h], ALIBI_TERMS)):
            pieces = jnp.where(qlane == t, piece, pieces)
        qs_ref[hr, DSA_DH:LANE] = pieces.astype(BF16)
    klane = lax.broadcasted_iota(jnp.int32, (ck, LANE - DSA_DH), 1)
    koff = lax.broadcasted_iota(jnp.int32, (ck, LANE - DSA_DH), 0).astype(F32)
    kaug_ref[:, DSA_DH:LANE] = jnp.where(klane < ALIBI_TERMS, koff, 0.0).astype(BF16)
    iwt_ref[...] = (iw_ref[...] * ((IDX_HEADS ** -0.5) * (IDX_DIM ** -0.5))).T[0:IDX_HEADS, :]

    def scores(c):
        k0 = pl.multiple_of(c * ck, ck)
        kaug_ref[:, 0:DSA_DH] = kk_ref[pl.ds(k0, ck), DKK_K0:DKK_K0 + DSA_DH]
        return _dot_nt(kaug_ref[...], qs_ref[...])

    sraw_ref[...] = scores(0)

    def index_chunk(c, carry):
        k0 = pl.multiple_of(c * ck, ck)
        ik = kk_ref[pl.ds(k0, ck), DKK_IK0:DKK_IK0 + IDX_DIM]
        acc = jnp.zeros((ck, tq), F32)
        for h in range(IDX_HEADS):
            rel = jnp.maximum(_dot_nt(ik, iqs_ref[h]), 0.0)
            acc = acc + rel * iwt_ref[h:h + 1, :]
        sc = jnp.where(k0 + kloc <= qpos, acc, NEG_INF)
        isc_ref[pl.ds(k0, ck), :] = sc
        ihi_ref[pl.ds(k0, ck), :] = sc.astype(HALF)
        return carry

    lax.fori_loop(0, nch, index_chunk, 0)

    thr_ref[...] = jnp.full(thr_ref.shape, F32_LOWEST, F32)

    def count(pred, ref=isc_ref, n_chunks=nch):
        def chunk(c, a):
            k0 = pl.multiple_of(c * ck, ck)
            blk = ref[pl.ds(k0, ck), :].reshape(grp, SUBLANE, tq)
            hit = pred(blk, k0).astype(F32).reshape(grp // DSA_NACC, DSA_NACC, SUBLANE, tq)
            return a + jnp.sum(hit, axis=0)
        a = lax.fori_loop(0, n_chunks, chunk, jnp.zeros((DSA_NACC, SUBLANE, tq), F32))
        return jnp.sum(a.reshape(DSA_NACC * SUBLANE, tq), axis=0, keepdims=True)

    def key_to_float(u):
        sk = u ^ INT_MIN
        return pltpu.bitcast(jnp.where(sk >= 0, sk, sk ^ 0x7FFFFFFF), F32)

    def rows8(v):
        return jnp.broadcast_to(v, (SUBLANE, tq))[None]

    def count_high(cand_hi):
        grp16 = ck // PACKED_SUBLANE
        cb = jnp.broadcast_to(cand_hi, (PACKED_SUBLANE, tq))[None]
        one = jnp.ones((), HALF)
        zero = jnp.zeros((), HALF)

        def chunk(c, a):
            k0 = pl.multiple_of(c * ck, ck)
            blk = ihi_ref[pl.ds(k0, ck), :].reshape(grp16, PACKED_SUBLANE, tq)
            hit = jnp.where(blk >= cb, one, zero).reshape(grp16 // DSA_NACC, DSA_NACC, PACKED_SUBLANE, tq)
            for g in range(grp16 // DSA_NACC):
                a = a + hit[g]
            return a
        a = lax.fori_loop(0, nch, chunk, jnp.zeros((DSA_NACC, PACKED_SUBLANE, tq), HALF))
        return jnp.sum(a.astype(F32).reshape(DSA_NACC * PACKED_SUBLANE, tq), axis=0, keepdims=True)

    @pl.when((i + 1) * tq > topk)
    def _search():
        def half_key(u):
            return u | jnp.where((u ^ INT_MIN) < 0, 0xFFFF, 0)

        def coarse_step(it, cur):
            cand = cur | lax.shift_left(jnp.int32(1), 31 - it)
            cnt = count_high(key_to_float(half_key(cand)).astype(HALF))
            return jnp.where(cnt >= topk, cand, cur)

        coarse = lax.fori_loop(0, 16, coarse_step, jnp.zeros((1, tq), jnp.int32))
        base = half_key(coarse) - 0x8000

        span = 1 << 17

        def count_ge(u):
            cf = rows8(key_to_float(u))
            return count(lambda blk, k0: blk >= cf)

        def refine(count_fn, n_base):
            def fine_step(it, st):
                off, n_ge = st
                cand = off | lax.shift_left(jnp.int32(1), 16 - it)
                cnt = count_fn(base + cand)
                ok = cnt >= topk
                return jnp.where(ok, cand, off), jnp.where(ok, cnt, n_ge)

            off, n_ge = lax.fori_loop(0, 17, fine_step, (jnp.zeros((1, tq), jnp.int32), n_base))
            thr_ref[...] = jnp.where(full, key_to_float(base + off), F32_LOWEST)
            nge_ref[...] = n_ge

        n_lo = count_ge(base)
        top8 = rows8(key_to_float(base + span))[0]
        ngrp = ck // (DSA_CGRP * SUBLANE)
        crows = 2 * ngrp * SUBLANE

        def extract(c, a):
            k0 = pl.multiple_of(c * ck, ck)
            blk = isc_ref[pl.ds(k0, ck), :].reshape(ngrp, DSA_CGRP, SUBLANE, tq)
            best = jnp.full((ngrp, SUBLANE, tq), NEG_INF, F32)
            second = best
            for r in range(DSA_CGRP):
                x = blk[:, r]
                below = x < top8
                a = a + jnp.where(below, 0.0, 1.0)
                v = jnp.where(below, x, NEG_INF)
                second = jnp.maximum(second, jnp.minimum(best, v))
                best = jnp.maximum(best, v)
            r0 = pl.multiple_of(c * crows, crows)
            cand_ref[pl.ds(r0, crows), :] = jnp.concatenate(
                [best.reshape(ngrp * SUBLANE, tq), second.reshape(ngrp * SUBLANE, tq)], axis=0)
            return a

        cand_ref[...] = jnp.full(cand_ref.shape, NEG_INF, F32)
        a = lax.fori_loop(0, nch, extract, jnp.zeros((ngrp, SUBLANE, tq), F32))
        n_top = jnp.sum(a.reshape(ngrp * SUBLANE, tq), axis=0, keepdims=True)

        def count_small(u):
            cf = rows8(key_to_float(u))
            return n_top + count(lambda blk, k0: blk >= cf, cand_ref, (nch * crows + ck - 1) // ck)

        lost = jnp.max(jnp.where(full, jnp.abs(n_lo - count_small(base)), 0.0))

        @pl.when(lost == 0.0)
        def _small():
            refine(count_small, n_lo)

        @pl.when(lost > 0.0)
        def _full():
            refine(count_ge, n_lo)

        thr = thr_ref[...]
        n_ge = nge_ref[...]
        excess = jnp.max(jnp.where(full, n_ge - topk, 0.0))

        @pl.when(excess > 0.0)
        def _ties():
            thr8 = rows8(thr)
            need = topk - count(lambda blk, k0: blk > thr8)
            nbits = (seq - 1).bit_length()
            kpos3 = lax.broadcasted_iota(jnp.int32, (grp, SUBLANE, tq), 0) * SUBLANE + \
                lax.broadcasted_iota(jnp.int32, (grp, SUBLANE, tq), 1)

            def pos_step(it, p):
                cand = p | lax.shift_left(jnp.int32(1), nbits - 1 - it)
                cand8 = rows8(cand)
                cnt = count(lambda blk, k0: (blk == thr8) & (k0 + kpos3 < cand8))
                return jnp.where(cnt < need, cand, p)

            last = lax.fori_loop(0, nbits, pos_step, jnp.zeros((1, tq), jnp.int32))

            def drop(c, carry):
                k0 = pl.multiple_of(c * ck, ck)
                blk = isc_ref[pl.ds(k0, ck), :]
                kill = (blk == thr) & (k0 + kloc > last) & full
                isc_ref[pl.ds(k0, ck), :] = jnp.where(kill, NEG_INF, blk)
                return carry

            lax.fori_loop(0, nch, drop, 0)

    m_ref[...] = jnp.full(m_ref.shape, NEG_INF, F32)
    l_ref[...] = jnp.zeros_like(l_ref)
    acc_ref[...] = jnp.zeros_like(acc_ref)
    thr = thr_ref[...]
    slope_row = jnp.concatenate([jnp.full((1, tq), sl, F32) for sl in ALIBI_SLOPES], axis=1)

    def attend(c, carry):
        k0 = pl.multiple_of(c * ck, ck)
        s5 = sraw_ref[...]
        s_next = scores(jnp.minimum(c + 1, nch - 1))
        sel = isc_ref[pl.ds(k0, ck), :] >= thr
        s5 = jnp.concatenate([jnp.where(sel, s5[:, h * tq:(h + 1) * tq], NEG_INF)
                              for h in range(DSA_HEADS)], axis=1)
        cvec = slope_row * k0.astype(F32)
        m_old = m_ref[...]
        m_new = jnp.maximum(m_old, jnp.max(s5, axis=0, keepdims=True) + cvec)
        m_safe = jnp.where(m_new == NEG_INF, 0.0, m_new)
        alpha = jnp.exp(m_old - m_safe)
        p = jnp.exp(s5 - (m_safe - cvec))
        l_ref[...] = alpha * l_ref[...] + jnp.sum(p, axis=0, keepdims=True)
        acc_ref[...] = alpha * acc_ref[...] + _dot(vt_ref[:, pl.ds(k0, ck)], p.astype(BF16))
        m_ref[...] = m_new
        sraw_ref[...] = s_next
        return carry

    lax.fori_loop(0, nch, attend, 0)

    ot_ref[DSA_W:, :] = jnp.zeros((DSA_WP - DSA_W, tq), F32)
    for h in range(DSA_HEADS):
        hs = slice(h * tq, (h + 1) * tq)
        o_lat = acc_ref[:, hs] / l_ref[:, hs]
        ot_ref[h * DSA_DH:(h + 1) * DSA_DH, :] = _dot(wuvt_ref[h], o_lat.astype(BF16))
    o_ref[...] = ot_ref[...].T[:, :DSA_W].astype(o_ref.dtype)


def _dsa(iq, dq, iw, dkk, dvt, wuvt, layer, batch, seq):
    tq = DSA_TQ
    nq = seq // tq
    topk = min(DSA_TOPK_MAX, seq // 4)
    return pl.pallas_call(
        functools.partial(_dsa_body, seq=seq, topk=topk),
        grid=(batch, nq),
        in_specs=[
            pl.BlockSpec((tq, iq.shape[1]), lambda b, i: (b * nq + i, 0)),
            pl.BlockSpec((tq, dq.shape[1]), lambda b, i: (b * nq + i, 0)),
            pl.BlockSpec((tq, iw.shape[1]), lambda b, i: (b * nq + i, 0)),
            pl.BlockSpec((seq, DKK_COLS), lambda b, i: (b, 0)),
            pl.BlockSpec((DSA_LATENT, seq), lambda b, i: (0, b)),
            _layer_block(wuvt, layer),
        ],
        out_specs=pl.BlockSpec((tq, DSA_W), lambda b, i: (b * nq + i, 0)),
        out_shape=jax.ShapeDtypeStruct((batch * seq, DSA_W), BF16),
        scratch_shapes=[
            pltpu.VMEM((seq, tq), F32),
            pltpu.VMEM((seq, tq), HALF),
            pltpu.VMEM((pl.cdiv(2 * seq // DSA_CGRP, DSA_CK) * DSA_CK, tq), F32),
            pltpu.VMEM((1, tq), F32),
            pltpu.VMEM((1, tq), F32),
            pltpu.VMEM((IDX_HEADS, tq, IDX_DIM), BF16),
            pltpu.VMEM((IDX_HEADS, tq), F32),
            pltpu.VMEM((DSA_HEADS * tq, LANE), BF16),
            pltpu.VMEM((DSA_CK, LANE), BF16),
            pltpu.VMEM((DSA_CK, DSA_HEADS * tq), F32),
            pltpu.VMEM((1, DSA_HEADS * tq), F32),
            pltpu.VMEM((1, DSA_HEADS * tq), F32),
            pltpu.VMEM((DSA_LATENT, DSA_HEADS * tq), F32),
            pltpu.VMEM((DSA_WP, tq), F32),
        ],
        compiler_params=_params(("parallel", "arbitrary")),
        name="dsa",
    )(iq, dq, iw, dkk, dvt, wuvt)


def _outproj_body(og_ref, od_ref, os_ref, h_ref, wg_ref, wd_ref, ws_ref, g_ref, b_ref, o_ref):
    f = _dot(og_ref[...], wg_ref[...]) + _dot(od_ref[...], wd_ref[...]) + _dot(os_ref[...], ws_ref[...])
    o_ref[...] = _ln(ALPHA * h_ref[...] + f, g_ref[...], b_ref[...])


def _memkv_body(mem_ref, w_ref, kv_ref):
    kv_ref[...] = _dot(mem_ref[...].astype(BF16), w_ref[...]).astype(kv_ref.dtype)


def _memkv(mem2d, wkv, layer, tm=512):
    m, d = mem2d.shape
    n = wkv.shape[-1]
    return pl.pallas_call(
        _memkv_body,
        grid=(m // tm,),
        in_specs=[pl.BlockSpec((tm, d), lambda i: (i, 0)), _layer_block(wkv, layer)],
        out_specs=pl.BlockSpec((tm, n), lambda i: (i, 0)),
        out_shape=jax.ShapeDtypeStruct((m, n), BF16),
        compiler_params=_params(("parallel",)),
        name="mem_kv",
    )(mem2d, wkv)


def _memattn_body(h_ref, kv_ref, wq_ref, wo_ref, g_ref, b_ref, o_ref, ctx_ref):
    tm, d = h_ref.shape
    dh = d // MEM_HEADS
    groups = [slice(r0, r0 + tm // 2) for r0 in (0, tm // 2)]
    qs = [_dot(h_ref[rs, :].astype(BF16), wq_ref[...]).astype(BF16) for rs in groups]

    def attend(rs, q):
        def scores(hd):
            cs = slice(hd * dh, (hd + 1) * dh)
            return _dot_nt(q[:, cs], kv_ref[:, cs]) * (dh ** -0.5)

        s = scores(0)
        ctx_prev = None
        for hd in range(MEM_HEADS):
            s_next = scores(hd + 1) if hd + 1 < MEM_HEADS else None
            s = s - jnp.max(s, axis=-1, keepdims=True)
            p = jnp.exp(s)
            p = p / jnp.sum(p, axis=-1, keepdims=True)
            ctx = _dot(p.astype(BF16), kv_ref[:, d + hd * dh:d + (hd + 1) * dh])
            if ctx_prev is not None:
                ctx_ref[rs, (hd - 1) * dh:hd * dh] = ctx_prev.astype(BF16)
            ctx_prev = ctx
            s = s_next
        ctx_ref[rs, (MEM_HEADS - 1) * dh:] = ctx_prev.astype(BF16)
        return _dot(ctx_ref[rs, :], wo_ref[...])

    fs = [attend(rs, q) for rs, q in zip(groups, qs)]
    for rs, f in zip(groups, fs):
        o_ref[rs, :] = _ln(ALPHA * h_ref[rs, :] + f, g_ref[...], b_ref[...])


MLP_FC = 512


def _mlp_body(h_ref, wu_ref, bu_ref, wd_ref, bd_ref, g_ref, b_ref, o_ref):
    h = h_ref[...]
    hb = h.astype(BF16)
    f = None
    for c0 in range(0, wu_ref.shape[1], MLP_FC):
        cs = slice(c0, c0 + MLP_FC)
        u = jnp.maximum(_dot(hb, wu_ref[:, cs]) + bu_ref[:, cs], 0.0)
        d = _dot((u * u).astype(BF16), wd_ref[cs, :])
        f = d if f is None else f + d
    o_ref[...] = _ln(ALPHA * h + (f + bd_ref[...]), g_ref[...], b_ref[...])


def _tail_body(og_ref, od_ref, os_ref, h_ref, kv_ref, wg_ref, wd_ref, ws_ref, g1_ref, b1_ref,
               wq_ref, wo_ref, g2_ref, b2_ref, wu_ref, bu_ref, wdn_ref, bdn_ref, g3_ref, b3_ref,
               o_ref, hs_ref, ctx_ref):
    _outproj_body(og_ref, od_ref, os_ref, h_ref, wg_ref, wd_ref, ws_ref, g1_ref, b1_ref, hs_ref)
    _memattn_body(hs_ref, kv_ref, wq_ref, wo_ref, g2_ref, b2_ref, hs_ref, ctx_ref)
    _mlp_body(hs_ref, wu_ref, bu_ref, wdn_ref, bdn_ref, g3_ref, b3_ref, o_ref)


def _tail(og, od, os_, h, kv, consts, layer, batch, seq, n_mem, tm=512):
    m, d = h.shape
    nt = seq // tm
    row = lambda a: pl.BlockSpec((tm, a.shape[1]), lambda bb, t: (bb * nt + t, 0))
    const = lambda a: _layer_block(a, layer, buffered=True)
    return pl.pallas_call(
        _tail_body,
        grid=(batch, nt),
        in_specs=[row(og), row(od), row(os_), row(h),
                  pl.BlockSpec((n_mem, 2 * d), lambda bb, t: (bb, 0))] + [const(a) for a in consts],
        out_specs=pl.BlockSpec((tm, d), lambda bb, t: (bb * nt + t, 0)),
        out_shape=jax.ShapeDtypeStruct((m, d), F32),
        scratch_shapes=[pltpu.VMEM((tm, d), F32), pltpu.VMEM((tm, d), BF16)],
        compiler_params=_params(("parallel", "parallel")),
        name="tail",
    )(og, od, os_, h, kv, *consts)


def _pad_heads(w, heads, width, padded):
    lead = w.shape[:-1]
    w = w.reshape(lead + (heads, width))
    w = jnp.pad(w, [(0, 0)] * len(lead) + [(0, 0), (0, padded - width)])
    return w.reshape(lead + (heads * padded,))


def _pad_last(w, n):
    return jnp.pad(w, [(0, 0)] * (w.ndim - 1) + [(0, n - w.shape[-1])])


def _split_w_in(w_in):
    sizes = (GLA_HEADS * GLA_DK, GLA_HEADS * GLA_DK, GLA_HEADS * GLA_DV, GLA_RANK, GLA_HEADS * GLA_DV,
             DSA_HEADS * DSA_DH, DSA_DH, DSA_LATENT, IDX_HEADS * IDX_DIM, IDX_DIM, IDX_HEADS,
             SB_HEADS * SB_DH, SB_HEADS * SB_DH, SB_HEADS * SB_DH)
    w_in = w_in.astype(BF16)
    parts = []
    o = 0
    for s in sizes:
        parts.append(w_in[..., o:o + s])
        o += s
    gq, gk, gv, glr, gg, dq, dk, dv, iq, ik, iw, sq, sk, sv = parts
    pieces = {
        "gla": jnp.concatenate([
            _pad_heads(gq, GLA_HEADS, GLA_DK, GLA_DKP), _pad_heads(gk, GLA_HEADS, GLA_DK, GLA_DKP),
            _pad_heads(gg, GLA_HEADS, GLA_DV, GLA_DVP)], axis=-1),
        "gv": _pad_heads(gv, GLA_HEADS, GLA_DV, GLA_DVP),
        "glr": _pad_last(glr, LANE),
        "iq": iq,
        "dkk": jnp.concatenate([ik, dk], axis=-1),
        "iw": _pad_last(iw, LANE),
        "dv": dv,
        "dq": _pad_last(dq, PAD_W),
        "sq": _pad_last(sq, PAD_W),
        "sk": _pad_last(sk, PAD_W),
        "sv": _pad_last(sv, PAD_W),
    }
    for name, width, _, _, _ in INPROJ_STREAMS:
        assert pieces[name].shape[-1] == width, name
    return jnp.concatenate([pieces[s[0]] for s in INPROJ_STREAMS], axis=-1)


def kernel(x, mem, ln_in_g, ln_in_b, w_in, gla_gate_w2, gla_gate_b, gla_norm_g, dsa_w_uv, w_out,
           ln_mix_g, ln_mix_b, w_mem_q, w_mem_kv, w_mem_o, ln_mem_g, ln_mem_b,
           w_up, b_up, w_down, b_down, ln_ffn_g, ln_ffn_b):
    batch, seq, d = x.shape
    n_mem = mem.shape[1]
    depth = w_in.shape[0]
    x2 = x.reshape(batch * seq, d)
    mem2 = mem.reshape(batch * n_mem, d)
    rows = lambda a: a.reshape(depth, 1, -1)
    gw, dw = GLA_HEADS * GLA_DV, DSA_HEADS * DSA_DH

    w_all = _split_w_in(w_in)
    w2p = jnp.pad(_pad_heads(gla_gate_w2, GLA_HEADS, GLA_DK, GLA_DKP),
                  ((0, 0), (0, LANE - GLA_RANK), (0, 0))).astype(BF16)
    gbp = rows(_pad_heads(gla_gate_b, GLA_HEADS, GLA_DK, GLA_DKP))
    ngp = rows(_pad_last(gla_norm_g, GLA_DVP))
    wuvt = jnp.swapaxes(dsa_w_uv, 2, 3).astype(BF16)
    wo = w_out.astype(BF16)
    wg = jnp.pad(wo[:, :gw].reshape(depth, GLA_HEADS, GLA_DV, d), ((0, 0), (0, 0), (0, GLA_DVP - GLA_DV), (0, 0)))
    wg = wg.reshape(depth, GLA_HEADS * GLA_DVP, d)
    wkv = w_mem_kv.astype(BF16)
    consts = (wg, wo[:, gw:gw + dw], wo[:, gw + dw:], rows(ln_mix_g), rows(ln_mix_b),
              w_mem_q.astype(BF16), w_mem_o.astype(BF16), rows(ln_mem_g), rows(ln_mem_b),
              w_up.astype(BF16), rows(b_up), w_down.astype(BF16), rows(b_down),
              rows(ln_ffn_g), rows(ln_ffn_b))

    h = None
    for l in range(depth):
        if l == 0:
            outs = _inproj(x2, w_all, l, ln=(ln_in_g, ln_in_b))
            h = outs[-1]
            outs = outs[:-1]
        else:
            outs = _inproj(h, w_all, l)
        p_gla, p_gv, p_gvt, p_glr, p_iq, p_dkk, p_iw, p_dvt, p_dq, p_sq, p_sk, p_svt = outs
        o_gla = _gla(p_gla, p_gv, p_gvt, p_glr, w2p, gbp, ngp, l, batch, seq)
        o_dsa = _dsa(p_iq, p_dq, p_iw, p_dkk, p_dvt, wuvt, l, batch, seq)
        o_sb = _sb(p_sq, p_sk, p_svt, batch, seq)
        kv = _memkv(mem2, wkv, l)
        h = _tail(o_gla, o_dsa, o_sb, h, kv, consts, l, batch, seq, n_mem)
    return h.reshape(batch, seq, d)
```

```python
import functools

import jax
import jax.numpy as jnp
import numpy as np
from jax import lax
from jax.experimental import pallas as pl
from jax.experimental.pallas import tpu as pltpu

F32 = jnp.float32
BF16 = jnp.bfloat16
HALF = jnp.bfloat16

DEPTH = 2
LN_EPS = 1e-5
GLA_HEADS, GLA_DK, GLA_DV, GLA_RANK, GLA_TAU, GLA_CHUNK = 4, 48, 96, 16, 16.0, 64
GLA_DKP, GLA_DVP = 64, 128
DSA_HEADS, DSA_DH, DSA_LATENT = 5, 64, 128
IDX_HEADS, IDX_DIM, DSA_TOPK_MAX = 8, 64, 256
SB_HEADS, SB_DH = 5, 64
MEM_HEADS = 4
ALPHA = (2.0 * DEPTH) ** 0.25
ALIBI_SLOPES = tuple(2.0 ** (-8.0 * (i + 1) / DSA_HEADS) for i in range(DSA_HEADS))
ALIBI_TERMS = 3
NEG_INF = float("-inf")
F32_LOWEST = float(jnp.finfo(jnp.float32).min)
INT_MIN = -(2 ** 31)

LANE = 128
SUBLANE = 8
PACKED_SUBLANE = 16
VMEM_LIMIT = 56 * 1024 * 1024


def _dot(a, b):
    return jnp.dot(a, b, preferred_element_type=F32)


def _dot_nt(a, b):
    return lax.dot_general(a, b, (((1,), (1,)), ((), ())), preferred_element_type=F32)


def _ln(x, g, b):
    mu = jnp.mean(x, axis=-1, keepdims=True)
    xc = x - mu
    var = jnp.mean(xc * xc, axis=-1, keepdims=True)
    return xc * lax.rsqrt(var + LN_EPS) * g + b


def _softplus(x):
    return jnp.maximum(x, 0.0) + jnp.log(1.0 + jnp.exp(-jnp.abs(x)))


def _split_dot_left(m, x, terms):
    out = None
    r = x
    for t in range(terms):
        xb = r.astype(BF16)
        d = _dot(m, xb)
        out = d if out is None else out + d
        if t + 1 < terms:
            r = r - xb.astype(F32)
    return out


def _bf16_terms(x, n):
    terms = []
    r = np.float32(x)
    for _ in range(n):
        t = np.float32(np.asarray(r).astype(jnp.bfloat16))
        terms.append(float(t))
        r = np.float32(r - t)
    return tuple(terms)


def _layer_block(a, layer, buffered=False):
    zeros = (0,) * (a.ndim - 1)
    kw = dict(pipeline_mode=pl.Buffered(1)) if buffered else {}
    return pl.BlockSpec((None,) + a.shape[1:], lambda *_: (layer,) + zeros, **kw)


def _params(sem):
    return pltpu.CompilerParams(dimension_semantics=sem, vmem_limit_bytes=VMEM_LIMIT)


PAD_W = 384
INPROJ_CHUNK = 512
INPROJ_STREAMS = (
    ("gla", 2 * GLA_HEADS * GLA_DKP + GLA_HEADS * GLA_DVP, F32, True, False),
    ("gv", GLA_HEADS * GLA_DVP, BF16, True, True),
    ("glr", LANE, BF16, True, False),
    ("iq", IDX_HEADS * IDX_DIM, BF16, True, False),
    ("dkk", IDX_DIM + DSA_DH, BF16, True, False),
    ("iw", LANE, F32, True, False),
    ("dv", DSA_LATENT, BF16, False, True),
    ("dq", PAD_W, BF16, True, False),
    ("sq", PAD_W, BF16, True, False),
    ("sk", PAD_W, BF16, True, False),
    ("sv", PAD_W, BF16, False, True),
)
INPROJ_COLS = sum(s[1] for s in INPROJ_STREAMS)


def _inproj_outputs():
    outs = []
    for si, (_, _, _, plain, tr) in enumerate(INPROJ_STREAMS):
        if plain:
            outs.append((si, False))
        if tr:
            outs.append((si, True))
    return outs


def _inproj_body(*refs, apply_ln):
    if apply_ln:
        x_ref, g_ref, b_ref, w_ref = refs[:4]
        o_refs = refs[4:]
    else:
        x_ref, w_ref = refs[:2]
        o_refs = refs[2:]
    x = x_ref[...]
    if apply_ln:
        x = _ln(x, g_ref[...], b_ref[...])
        o_refs[-1][...] = x
    xb = x.astype(BF16)
    starts = np.cumsum([0] + [s[1] for s in INPROJ_STREAMS])
    outs = _inproj_outputs()
    for c0 in range(0, INPROJ_COLS, INPROJ_CHUNK):
        c1 = c0 + INPROJ_CHUNK
        y = _dot(xb, w_ref[:, c0:c1])
        for o_ref, (si, tr) in zip(o_refs, outs):
            a, b = max(int(starts[si]), c0), min(int(starts[si + 1]), c1)
            if a >= b:
                continue
            part = y[:, a - c0:b - c0]
            lo, hi = a - int(starts[si]), b - int(starts[si])
            if tr:
                o_ref[lo:hi, :] = part.T.astype(o_ref.dtype)
            else:
                o_ref[:, lo:hi] = part.astype(o_ref.dtype)


def _inproj(x, w_all, layer, ln=None, tm=512):
    m, d = x.shape
    apply_ln = ln is not None
    in_specs = [pl.BlockSpec((tm, d), lambda i: (i, 0))]
    args = [x]
    if apply_ln:
        in_specs += [pl.BlockSpec((1, d), lambda i: (0, 0))] * 2
        args += [ln[0].reshape(1, d), ln[1].reshape(1, d)]
    in_specs.append(_layer_block(w_all, layer))
    args.append(w_all)
    out_shape, out_specs = [], []
    for si, tr in _inproj_outputs():
        _, width, dt, _, _ = INPROJ_STREAMS[si]
        if tr:
            out_shape.append(jax.ShapeDtypeStruct((width, m), dt))
            out_specs.append(pl.BlockSpec((width, tm), lambda i: (0, i)))
        else:
            out_shape.append(jax.ShapeDtypeStruct((m, width), dt))
            out_specs.append(pl.BlockSpec((tm, width), lambda i: (i, 0)))
    if apply_ln:
        out_shape.append(jax.ShapeDtypeStruct((m, d), F32))
        out_specs.append(pl.BlockSpec((tm, d), lambda i: (i, 0)))
    return pl.pallas_call(
        functools.partial(_inproj_body, apply_ln=apply_ln),
        grid=(m // tm,),
        in_specs=in_specs,
        out_specs=out_specs,
        out_shape=out_shape,
        compiler_params=_params(("parallel",)),
        name="inproj_ln" if apply_ln else "inproj",
    )(*args)


GLA_KW = GLA_HEADS * GLA_DKP
GLA_VW = GLA_HEADS * GLA_DVP
GLA_Q0, GLA_K0, GLA_G0 = 0, GLA_KW, 2 * GLA_KW
GLA_COLS = GLA_G0 + GLA_VW
GLA_TL = 256
GLA_NC = GLA_TL // GLA_CHUNK


def _gla_body(x_ref, v_ref, vt_ref, r_ref, w2_ref, gb_ref, ng_ref, o_ref, s_ref):
    @pl.when(pl.program_id(1) == 0)
    def _():
        s_ref[...] = jnp.zeros_like(s_ref)

    tl, c, nc = GLA_TL, GLA_CHUNK, GLA_NC
    row = lax.broadcasted_iota(jnp.int32, (tl, tl), 0)
    col = lax.broadcasted_iota(jnp.int32, (tl, tl), 1)
    rch = jnp.floor(row.astype(F32) * (1.0 / c))
    cch = jnp.floor(col.astype(F32) * (1.0 / c))
    causal = (rch == cch) & (col <= row)
    tri = causal.astype(BF16)
    heads = range(GLA_HEADS)
    ksl = [slice(h * GLA_DKP, (h + 1) * GLA_DKP) for h in heads]
    vsl = [slice(h * GLA_DVP, (h + 1) * GLA_DVP) for h in heads]

    xg = _dot(r_ref[...], w2_ref[...]) + gb_ref[...]
    log_a = -_softplus(-xg) / GLA_TAU
    bcum = _split_dot_left(tri, log_a, 3)
    krow = lax.broadcasted_iota(jnp.int32, (tl, GLA_KW), 0)
    blast = bcum[tl - 1:tl, :]
    for j in range(nc - 2, -1, -1):
        blast = jnp.where(krow < (j + 1) * c, bcum[(j + 1) * c - 1:(j + 1) * c, :], blast)
    q = x_ref[:, GLA_Q0:GLA_Q0 + GLA_KW]
    k = x_ref[:, GLA_K0:GLA_K0 + GLA_KW]
    q_t = (q * (GLA_DK ** -0.5) * jnp.exp(bcum)).astype(BF16)
    k_t = (k * jnp.exp(-bcum)).astype(BF16)
    k_end = (k * jnp.exp(blast - bcum)).astype(BF16)
    dec = [jnp.exp(bcum[(j + 1) * c - 1:(j + 1) * c, :]) for j in range(nc)]

    lane_chunk = jnp.floor(lax.broadcasted_iota(jnp.int32, (GLA_DVP, tl), 1).astype(F32) * (1.0 / c))
    o_intra, ds = [], []
    for h in heads:
        sc = jnp.where(causal, _dot_nt(q_t[:, ksl[h]], k_t[:, ksl[h]]), 0.0)
        o_intra.append(_dot(sc.astype(BF16), v_ref[:, vsl[h]]))
        vt = vt_ref[vsl[h], :]
        zero = jnp.zeros_like(vt)
        vt_by_chunk = jnp.concatenate([jnp.where(lane_chunk == float(j), vt, zero) for j in range(nc)], axis=0)
        ds.append(_dot(vt_by_chunk, k_end[:, ksl[h]]))

    ng = ng_ref[...]
    for h in heads:
        s = s_ref[h]
        states = []
        for j in range(nc):
            states.append(s.astype(BF16))
            s = s * dec[j][:, ksl[h]] + ds[h][j * GLA_DVP:(j + 1) * GLA_DVP, :]
        s_ref[h] = s
        inter_all = _dot_nt(q_t[:, ksl[h]], jnp.concatenate(states, axis=0))
        o_inter = jnp.concatenate(
            [inter_all[j * c:(j + 1) * c, j * GLA_DVP:(j + 1) * GLA_DVP] for j in range(nc)], axis=0)
        o = o_intra[h] + o_inter
        ms = jnp.sum(o * o, axis=-1, keepdims=True) * (1.0 / GLA_DV)
        on = o * lax.rsqrt(ms + LN_EPS) * ng
        g = x_ref[:, GLA_G0 + h * GLA_DVP:GLA_G0 + (h + 1) * GLA_DVP]
        o_ref[:, vsl[h]] = (on * (g / (1.0 + jnp.exp(-g)))).astype(o_ref.dtype)


def _gla(x32, v, vt, glr, w2p, gbp, ngp, layer, batch, seq):
    tl = GLA_TL
    nt = seq // tl
    const = lambda a: _layer_block(a, layer)
    return pl.pallas_call(
        _gla_body,
        grid=(batch, nt),
        in_specs=[
            pl.BlockSpec((tl, GLA_COLS), lambda b, t: (b * nt + t, 0)),
            pl.BlockSpec((tl, GLA_VW), lambda b, t: (b * nt + t, 0)),
            pl.BlockSpec((GLA_VW, tl), lambda b, t: (0, b * nt + t)),
            pl.BlockSpec((tl, LANE), lambda b, t: (b * nt + t, 0)),
            const(w2p), const(gbp), const(ngp),
        ],
        out_specs=pl.BlockSpec((tl, GLA_VW), lambda b, t: (b * nt + t, 0)),
        out_shape=jax.ShapeDtypeStruct((batch * seq, GLA_VW), BF16),
        scratch_shapes=[pltpu.VMEM((GLA_HEADS, GLA_DVP, GLA_DKP), F32)],
        compiler_params=_params(("arbitrary", "arbitrary")),
        name="gla",
    )(x32, v, vt, glr, w2p, gbp, ngp)


SB_TQ = 256
SB_CK = 256
SB_W = SB_HEADS * SB_DH
SB_WP = 384


def _sb_body(q_ref, k_ref, vt_ref, o_ref, qs_ref, acc_ref, run_ref):
    tq, ck = SB_TQ, SB_CK
    i = pl.program_id(1)
    c_last = (i + 1) * (tq // ck) - 1
    kloc = lax.broadcasted_iota(jnp.int32, (ck, tq), 0)
    qloc = lax.broadcasted_iota(jnp.int32, (ck, tq), 1)
    qpos = i * tq + qloc
    later = (lax.broadcasted_iota(jnp.int32, (ck, ck), 1) > lax.broadcasted_iota(jnp.int32, (ck, ck), 0)).astype(BF16)
    for h in range(SB_HEADS):
        qs_ref[h] = q_ref[:, h * SB_DH:(h + 1) * SB_DH] * (SB_DH ** -0.5)
    acc_ref[...] = jnp.zeros_like(acc_ref)
    run_ref[...] = jnp.zeros_like(run_ref)

    heads = range(SB_HEADS)
    hsl = [slice(h * SB_DH, (h + 1) * SB_DH) for h in heads]

    def chunk(c, masked):
        k0 = pl.multiple_of(c * ck, ck)
        zs = [_dot_nt(k_ref[pl.ds(k0, ck), hsl[h]], qs_ref[h]) for h in heads]
        runs = run_ref[...]
        sps = [_softplus(z) for z in zs]
        l1s = [-sp for sp in sps]
        if masked:
            valid = k0 + kloc < qpos
            l1s = [jnp.where(valid, l1, 0.0) for l1 in l1s]
        suf = _split_dot_left(later, jnp.concatenate(l1s, axis=1), 2)
        new_runs = []
        for h in heads:
            w = jnp.exp(zs[h] - sps[h] + suf[:, h * tq:(h + 1) * tq] + runs[h:h + 1, :])
            if masked:
                w = jnp.where(valid, w, 0.0)
            acc_ref[hsl[h], :] += _dot(vt_ref[hsl[h], pl.ds(k0, ck)], w.astype(BF16))
            new_runs.append(runs[h:h + 1, :] + jnp.sum(l1s[h], axis=0, keepdims=True))
        new_runs = jnp.concatenate(new_runs, axis=0)
        run_ref[0:SB_HEADS, :] = new_runs
        return (jnp.max(jnp.exp(new_runs + 2.0)) > 0.0).astype(jnp.int32)

    n_diag = tq // ck
    alive = jnp.int32(1)
    for j in range(n_diag):
        alive = chunk(c_last - j, True)

    def cond(st):
        step, alive = st
        return (step <= c_last) & (alive > 0)

    def body(st):
        step, _ = st
        return step + 1, chunk(c_last - step, False)

    lax.while_loop(cond, body, (jnp.int32(n_diag), alive))
    o_ref[...] = acc_ref[...].T[:, :SB_W].astype(o_ref.dtype)


def _sb(q, k, vt, batch, seq):
    tq = SB_TQ
    nq = seq // tq
    return pl.pallas_call(
        _sb_body,
        grid=(batch, nq),
        in_specs=[
            pl.BlockSpec((tq, q.shape[1]), lambda b, i: (b * nq + i, 0)),
            pl.BlockSpec((seq, k.shape[1]), lambda b, i: (b, 0)),
            pl.BlockSpec((vt.shape[0], seq), lambda b, i: (0, b)),
        ],
        out_specs=pl.BlockSpec((tq, SB_W), lambda b, i: (b * nq + i, 0)),
        out_shape=jax.ShapeDtypeStruct((batch * seq, SB_W), BF16),
        scratch_shapes=[
            pltpu.VMEM((SB_HEADS, tq, SB_DH), BF16),
            pltpu.VMEM((SB_WP, tq), F32),
            pltpu.VMEM((SUBLANE, tq), F32),
        ],
        compiler_params=_params(("parallel", "arbitrary")),
        name="sb",
    )(q, k, vt)


DSA_TQ = 256
DSA_CK = 256
DSA_NACC = 4
DSA_CGRP = 8
DSA_W = DSA_HEADS * DSA_DH
DSA_WP = 384
DKK_IK0, DKK_K0 = 0, IDX_DIM
DKK_COLS = IDX_DIM + DSA_DH


def _dsa_body(iq_ref, dq_ref, iw_ref, kk_ref, vt_ref, wuvt_ref, o_ref,
              isc_ref, ihi_ref, cand_ref, thr_ref, nge_ref, iqs_ref, iwt_ref, qs_ref, kaug_ref, sraw_ref, pbuf_ref,
              m_ref, l_ref, acc_ref, ot_ref,
              *, seq, topk):
    tq, ck = DSA_TQ, DSA_CK
    grp = ck // SUBLANE
    i = pl.program_id(1)
    nch = (i + 1) * (tq // ck)
    kloc = lax.broadcasted_iota(jnp.int32, (ck, tq), 0)
    qloc = lax.broadcasted_iota(jnp.int32, (ck, tq), 1)
    qpos = i * tq + qloc
    rowpos = i * tq + lax.broadcasted_iota(jnp.int32, (1, tq), 1)
    full = rowpos >= topk

    for h in range(IDX_HEADS):
        iqs_ref[h] = iq_ref[:, h * IDX_DIM:(h + 1) * IDX_DIM]
    qlane = lax.broadcasted_iota(jnp.int32, (tq, LANE - DSA_DH), 1)
    for h in range(DSA_HEADS):
        hr = slice(h * tq, (h + 1) * tq)
        qs_ref[hr, 0:DSA_DH] = dq_ref[:, h * DSA_DH:(h + 1) * DSA_DH] * (DSA_DH ** -0.5)
        pieces = jnp.zeros((tq, LANE - DSA_DH), F32)
        for t, piece in enumerate(_bf16_terms(ALIBI_SLOPES[h], ALIBI_TERMS)):
            pieces = jnp.where(qlane == t, piece, pieces)
        qs_ref[hr, DSA_DH:LANE] = pieces.astype(BF16)
    klane = lax.broadcasted_iota(jnp.int32, (ck, LANE - DSA_DH), 1)
    koff = lax.broadcasted_iota(jnp.int32, (ck, LANE - DSA_DH), 0).astype(F32)
    kaug_ref[:, DSA_DH:LANE] = jnp.where(klane < ALIBI_TERMS, koff, 0.0).astype(BF16)
    iwt_ref[...] = (iw_ref[...] * ((IDX_HEADS ** -0.5) * (IDX_DIM ** -0.5))).T[0:IDX_HEADS, :]

    def scores(c):
        k0 = pl.multiple_of(c * ck, ck)
        kaug_ref[:, 0:DSA_DH] = kk_ref[pl.ds(k0, ck), DKK_K0:DKK_K0 + DSA_DH]
        return _dot_nt(kaug_ref[...], qs_ref[...])

    sraw_ref[...] = scores(0)

    def index_chunk(c, carry):
        k0 = pl.multiple_of(c * ck, ck)
        ik = kk_ref[pl.ds(k0, ck), DKK_IK0:DKK_IK0 + IDX_DIM]
        raw = [_dot_nt(ik, iqs_ref[h]) for h in range(IDX_HEADS)]
        acc = jnp.zeros((ck, tq), F32)
        for h in range(IDX_HEADS):
            acc = acc + jnp.maximum(raw[h], 0.0) * iwt_ref[h:h + 1, :]
        sc = jnp.where(k0 + kloc <= qpos, acc, NEG_INF)
        isc_ref[pl.ds(k0, ck), :] = sc
        ihi_ref[pl.ds(k0, ck), :] = sc.astype(HALF)
        return carry

    lax.fori_loop(0, nch, index_chunk, 0)

    thr_ref[...] = jnp.full(thr_ref.shape, F32_LOWEST, F32)

    def count(pred, ref=isc_ref, n_chunks=nch):
        def chunk(c, a):
            k0 = pl.multiple_of(c * ck, ck)
            blk = ref[pl.ds(k0, ck), :].reshape(grp, SUBLANE, tq)
            hit = pred(blk, k0).astype(F32).reshape(grp // DSA_NACC, DSA_NACC, SUBLANE, tq)
            return a + jnp.sum(hit, axis=0)
        a = lax.fori_loop(0, n_chunks, chunk, jnp.zeros((DSA_NACC, SUBLANE, tq), F32))
        return jnp.sum(a.reshape(DSA_NACC * SUBLANE, tq), axis=0, keepdims=True)

    def key_to_float(u):
        sk = u ^ INT_MIN
        return pltpu.bitcast(jnp.where(sk >= 0, sk, sk ^ 0x7FFFFFFF), F32)

    def rows8(v):
        return jnp.broadcast_to(v, (SUBLANE, tq))[None]

    def count_high(cand_hi):
        grp16 = ck // PACKED_SUBLANE
        cb = jnp.broadcast_to(cand_hi, (PACKED_SUBLANE, tq))[None]
        one = jnp.ones((), HALF)
        zero = jnp.zeros((), HALF)

        def chunk(c, a):
            k0 = pl.multiple_of(c * ck, ck)
            blk = ihi_ref[pl.ds(k0, ck), :].reshape(grp16, PACKED_SUBLANE, tq)
            hit = jnp.where(blk >= cb, one, zero).reshape(grp16 // DSA_NACC, DSA_NACC, PACKED_SUBLANE, tq)
            for g in range(grp16 // DSA_NACC):
                a = a + hit[g]
            return a
        a = lax.fori_loop(0, nch, chunk, jnp.zeros((DSA_NACC, PACKED_SUBLANE, tq), HALF))
        return jnp.sum(a.astype(F32).reshape(DSA_NACC * PACKED_SUBLANE, tq), axis=0, keepdims=True)

    @pl.when((i + 1) * tq > topk)
    def _search():
        def half_key(u):
            return u | jnp.where((u ^ INT_MIN) < 0, 0xFFFF, 0)

        def coarse_step(it, cur):
            cand = cur | lax.shift_left(jnp.int32(1), 31 - it)
            cnt = count_high(key_to_float(half_key(cand)).astype(HALF))
            return jnp.where(cnt >= topk, cand, cur)

        coarse = lax.fori_loop(0, 16, coarse_step, jnp.zeros((1, tq), jnp.int32))
        base = half_key(coarse) - 0x8000

        span = 1 << 17

        def count_ge(u):
            cf = rows8(key_to_float(u))
            return count(lambda blk, k0: blk >= cf)

        def refine(count_fn, n_base):
            def fine_step(it, st):
                off, n_ge = st
                cand = off | lax.shift_left(jnp.int32(1), 16 - it)
                cnt = count_fn(base + cand)
                ok = cnt >= topk
                return jnp.where(ok, cand, off), jnp.where(ok, cnt, n_ge)

            off, n_ge = lax.fori_loop(0, 17, fine_step, (jnp.zeros((1, tq), jnp.int32), n_base))
            thr_ref[...] = jnp.where(full, key_to_float(base + off), F32_LOWEST)
            nge_ref[...] = n_ge

        n_lo = count_ge(base)
        top8 = rows8(key_to_float(base + span))[0]
        ngrp = ck // (DSA_CGRP * SUBLANE)
        crows = 2 * ngrp * SUBLANE

        def extract(c, a):
            k0 = pl.multiple_of(c * ck, ck)
            blk = isc_ref[pl.ds(k0, ck), :].reshape(ngrp, DSA_CGRP, SUBLANE, tq)
            best = jnp.full((ngrp, SUBLANE, tq), NEG_INF, F32)
            second = best
            for r in range(DSA_CGRP):
                x = blk[:, r]
                below = x < top8
                a = a + jnp.where(below, 0.0, 1.0)
                v = jnp.where(below, x, NEG_INF)
                second = jnp.maximum(second, jnp.minimum(best, v))
                best = jnp.maximum(best, v)
            r0 = pl.multiple_of(c * crows, crows)
            cand_ref[pl.ds(r0, crows), :] = jnp.concatenate(
                [best.reshape(ngrp * SUBLANE, tq), second.reshape(ngrp * SUBLANE, tq)], axis=0)
            return a

        cand_ref[...] = jnp.full(cand_ref.shape, NEG_INF, F32)
        a = lax.fori_loop(0, nch, extract, jnp.zeros((ngrp, SUBLANE, tq), F32))
        n_top = jnp.sum(a.reshape(ngrp * SUBLANE, tq), axis=0, keepdims=True)

        def count_small(u):
            cf = rows8(key_to_float(u))
            return n_top + count(lambda blk, k0: blk >= cf, cand_ref, (nch * crows + ck - 1) // ck)

        lost = jnp.max(jnp.where(full, jnp.abs(n_lo - count_small(base)), 0.0))

        @pl.when(lost == 0.0)
        def _small():
            refine(count_small, n_lo)

        @pl.when(lost > 0.0)
        def _full():
            refine(count_ge, n_lo)

        thr = thr_ref[...]
        n_ge = nge_ref[...]
        excess = jnp.max(jnp.where(full, n_ge - topk, 0.0))

        @pl.when(excess > 0.0)
        def _ties():
            thr8 = rows8(thr)
            need = topk - count(lambda blk, k0: blk > thr8)
            nbits = (seq - 1).bit_length()
            kpos3 = lax.broadcasted_iota(jnp.int32, (grp, SUBLANE, tq), 0) * SUBLANE + \
                lax.broadcasted_iota(jnp.int32, (grp, SUBLANE, tq), 1)

            def pos_step(it, p):
                cand = p | lax.shift_left(jnp.int32(1), nbits - 1 - it)
                cand8 = rows8(cand)
                cnt = count(lambda blk, k0: (blk == thr8) & (k0 + kpos3 < cand8))
                return jnp.where(cnt < need, cand, p)

            last = lax.fori_loop(0, nbits, pos_step, jnp.zeros((1, tq), jnp.int32))

            def drop(c, carry):
                k0 = pl.multiple_of(c * ck, ck)
                blk = isc_ref[pl.ds(k0, ck), :]
                kill = (blk == thr) & (k0 + kloc > last) & full
                isc_ref[pl.ds(k0, ck), :] = jnp.where(kill, NEG_INF, blk)
                return carry

            lax.fori_loop(0, nch, drop, 0)

    m_ref[...] = jnp.full(m_ref.shape, NEG_INF, F32)
    l_ref[...] = jnp.zeros_like(l_ref)
    acc_ref[...] = jnp.zeros_like(acc_ref)
    thr = thr_ref[...]
    slope_row = jnp.concatenate([jnp.full((1, tq), sl, F32) for sl in ALIBI_SLOPES], axis=1)

    def weighted_values(c):
        return _dot(vt_ref[:, pl.ds(pl.multiple_of(c * ck, ck), ck)], pbuf_ref[...])

    pbuf_ref[...] = jnp.zeros_like(pbuf_ref)

    def attend(c, carry):
        k0 = pl.multiple_of(c * ck, ck)
        s5 = sraw_ref[...]
        pv_prev = weighted_values(jnp.maximum(c - 1, 0))
        s_next = scores(jnp.minimum(c + 1, nch - 1))
        sel = isc_ref[pl.ds(k0, ck), :] >= thr
        s5 = jnp.concatenate([jnp.where(sel, s5[:, h * tq:(h + 1) * tq], NEG_INF)
                              for h in range(DSA_HEADS)], axis=1)
        cvec = slope_row * k0.astype(F32)
        m_old = m_ref[...]
        m_new = jnp.maximum(m_old, jnp.max(s5, axis=0, keepdims=True) + cvec)
        m_safe = jnp.where(m_new == NEG_INF, 0.0, m_new)
        alpha = jnp.exp(m_old - m_safe)
        p = jnp.exp(s5 - (m_safe - cvec))
        l_ref[...] = alpha * l_ref[...] + jnp.sum(p, axis=0, keepdims=True)
        acc_ref[...] = alpha * (acc_ref[...] + pv_prev)
        m_ref[...] = m_new
        sraw_ref[...] = s_next
        pbuf_ref[...] = p.astype(BF16)
        return carry

    lax.fori_loop(0, nch, attend, 0)
    acc_ref[...] += weighted_values(nch - 1)

    ot_ref[DSA_W:, :] = jnp.zeros((DSA_WP - DSA_W, tq), F32)
    for h in range(DSA_HEADS):
        hs = slice(h * tq, (h + 1) * tq)
        o_lat = acc_ref[:, hs] / l_ref[:, hs]
        ot_ref[h * DSA_DH:(h + 1) * DSA_DH, :] = _dot(wuvt_ref[h], o_lat.astype(BF16))
    o_ref[...] = ot_ref[...].T[:, :DSA_W].astype(o_ref.dtype)


def _dsa(iq, dq, iw, dkk, dvt, wuvt, layer, batch, seq):
    tq = DSA_TQ
    nq = seq // tq
    topk = min(DSA_TOPK_MAX, seq // 4)
    return pl.pallas_call(
        functools.partial(_dsa_body, seq=seq, topk=topk),
        grid=(batch, nq),
        in_specs=[
            pl.BlockSpec((tq, iq.shape[1]), lambda b, i: (b * nq + i, 0)),
            pl.BlockSpec((tq, dq.shape[1]), lambda b, i: (b * nq + i, 0)),
            pl.BlockSpec((tq, iw.shape[1]), lambda b, i: (b * nq + i, 0)),
            pl.BlockSpec((seq, DKK_COLS), lambda b, i: (b, 0)),
            pl.BlockSpec((DSA_LATENT, seq), lambda b, i: (0, b)),
            _layer_block(wuvt, layer),
        ],
        out_specs=pl.BlockSpec((tq, DSA_W), lambda b, i: (b * nq + i, 0)),
        out_shape=jax.ShapeDtypeStruct((batch * seq, DSA_W), BF16),
        scratch_shapes=[
            pltpu.VMEM((seq, tq), F32),
            pltpu.VMEM((seq, tq), HALF),
            pltpu.VMEM((pl.cdiv(2 * seq // DSA_CGRP, DSA_CK) * DSA_CK, tq), F32),
            pltpu.VMEM((1, tq), F32),
            pltpu.VMEM((1, tq), F32),
            pltpu.VMEM((IDX_HEADS, tq, IDX_DIM), BF16),
            pltpu.VMEM((IDX_HEADS, tq), F32),
            pltpu.VMEM((DSA_HEADS * tq, LANE), BF16),
            pltpu.VMEM((DSA_CK, LANE), BF16),
            pltpu.VMEM((DSA_CK, DSA_HEADS * tq), F32),
            pltpu.VMEM((DSA_CK, DSA_HEADS * tq), BF16),
            pltpu.VMEM((1, DSA_HEADS * tq), F32),
            pltpu.VMEM((1, DSA_HEADS * tq), F32),
            pltpu.VMEM((DSA_LATENT, DSA_HEADS * tq), F32),
            pltpu.VMEM((DSA_WP, tq), F32),
        ],
        compiler_params=_params(("parallel", "arbitrary")),
        name="dsa",
    )(iq, dq, iw, dkk, dvt, wuvt)


def _outproj_body(og_ref, od_ref, os_ref, h_ref, wg_ref, wd_ref, ws_ref, g_ref, b_ref, o_ref):
    f = _dot(og_ref[...], wg_ref[...]) + _dot(od_ref[...], wd_ref[...]) + _dot(os_ref[...], ws_ref[...])
    o_ref[...] = _ln(ALPHA * h_ref[...] + f, g_ref[...], b_ref[...])


def _memkv_body(mem_ref, w_ref, kv_ref):
    kv_ref[...] = _dot(mem_ref[...].astype(BF16), w_ref[...]).astype(kv_ref.dtype)


def _memkv(mem2d, wkv, layer, tm=512):
    m, d = mem2d.shape
    n = wkv.shape[-1]
    return pl.pallas_call(
        _memkv_body,
        grid=(m // tm,),
        in_specs=[pl.BlockSpec((tm, d), lambda i: (i, 0)), _layer_block(wkv, layer)],
        out_specs=pl.BlockSpec((tm, n), lambda i: (i, 0)),
        out_shape=jax.ShapeDtypeStruct((m, n), BF16),
        compiler_params=_params(("parallel",)),
        name="mem_kv",
    )(mem2d, wkv)


def _memattn_body(h_ref, kv_ref, wq_ref, wo_ref, g_ref, b_ref, o_ref, ctx_ref):
    tm, d = h_ref.shape
    dh = d // MEM_HEADS
    groups = [slice(r0, r0 + tm // 2) for r0 in (0, tm // 2)]
    qs = [_dot(h_ref[rs, :].astype(BF16), wq_ref[...]).astype(BF16) for rs in groups]

    def attend(rs, q):
        def scores(hd):
            cs = slice(hd * dh, (hd + 1) * dh)
            return _dot_nt(q[:, cs], kv_ref[:, cs]) * (dh ** -0.5)

        s = scores(0)
        ctx_prev = None
        for hd in range(MEM_HEADS):
            s_next = scores(hd + 1) if hd + 1 < MEM_HEADS else None
            s = s - jnp.max(s, axis=-1, keepdims=True)
            p = jnp.exp(s)
            p = p / jnp.sum(p, axis=-1, keepdims=True)
            ctx = _dot(p.astype(BF16), kv_ref[:, d + hd * dh:d + (hd + 1) * dh])
            if ctx_prev is not None:
                ctx_ref[rs, (hd - 1) * dh:hd * dh] = ctx_prev.astype(BF16)
            ctx_prev = ctx
            s = s_next
        ctx_ref[rs, (MEM_HEADS - 1) * dh:] = ctx_prev.astype(BF16)
        return _dot(ctx_ref[rs, :], wo_ref[...])

    fs = [attend(rs, q) for rs, q in zip(groups, qs)]
    for rs, f in zip(groups, fs):
        o_ref[rs, :] = _ln(ALPHA * h_ref[rs, :] + f, g_ref[...], b_ref[...])


MLP_FC = 512


def _mlp_body(h_ref, wu_ref, bu_ref, wd_ref, bd_ref, g_ref, b_ref, o_ref):
    h = h_ref[...]
    hb = h.astype(BF16)
    f = None
    for c0 in range(0, wu_ref.shape[1], MLP_FC):
        cs = slice(c0, c0 + MLP_FC)
        u = jnp.maximum(_dot(hb, wu_ref[:, cs]) + bu_ref[:, cs], 0.0)
        d = _dot((u * u).astype(BF16), wd_ref[cs, :])
        f = d if f is None else f + d
    o_ref[...] = _ln(ALPHA * h + (f + bd_ref[...]), g_ref[...], b_ref[...])


def _tail_body(og_ref, od_ref, os_ref, h_ref, kv_ref, wg_ref, wd_ref, ws_ref, g1_ref, b1_ref,
               wq_ref, wo_ref, g2_ref, b2_ref, wu_ref, bu_ref, wdn_ref, bdn_ref, g3_ref, b3_ref,
               o_ref, hs_ref, ctx_ref):
    _outproj_body(og_ref, od_ref, os_ref, h_ref, wg_ref, wd_ref, ws_ref, g1_ref, b1_ref, hs_ref)
    _memattn_body(hs_ref, kv_ref, wq_ref, wo_ref, g2_ref, b2_ref, hs_ref, ctx_ref)
    _mlp_body(hs_ref, wu_ref, bu_ref, wdn_ref, bdn_ref, g3_ref, b3_ref, o_ref)


def _tail(og, od, os_, h, kv, consts, layer, batch, seq, n_mem, tm=512):
    m, d = h.shape
    nt = seq // tm
    row = lambda a: pl.BlockSpec((tm, a.shape[1]), lambda bb, t: (bb * nt + t, 0))
    const = lambda a: _layer_block(a, layer, buffered=True)
    return pl.pallas_call(
        _tail_body,
        grid=(batch, nt),
        in_specs=[row(og), row(od), row(os_), row(h),
                  pl.BlockSpec((n_mem, 2 * d), lambda bb, t: (bb, 0))] + [const(a) for a in consts],
        out_specs=pl.BlockSpec((tm, d), lambda bb, t: (bb * nt + t, 0)),
        out_shape=jax.ShapeDtypeStruct((m, d), F32),
        scratch_shapes=[pltpu.VMEM((tm, d), F32), pltpu.VMEM((tm, d), BF16)],
        compiler_params=_params(("parallel", "parallel")),
        name="tail",
    )(og, od, os_, h, kv, *consts)


def _pad_heads(w, heads, width, padded):
    lead = w.shape[:-1]
    w = w.reshape(lead + (heads, width))
    w = jnp.pad(w, [(0, 0)] * len(lead) + [(0, 0), (0, padded - width)])
    return w.reshape(lead + (heads * padded,))


def _pad_last(w, n):
    return jnp.pad(w, [(0, 0)] * (w.ndim - 1) + [(0, n - w.shape[-1])])


def _split_w_in(w_in):
    sizes = (GLA_HEADS * GLA_DK, GLA_HEADS * GLA_DK, GLA_HEADS * GLA_DV, GLA_RANK, GLA_HEADS * GLA_DV,
             DSA_HEADS * DSA_DH, DSA_DH, DSA_LATENT, IDX_HEADS * IDX_DIM, IDX_DIM, IDX_HEADS,
             SB_HEADS * SB_DH, SB_HEADS * SB_DH, SB_HEADS * SB_DH)
    w_in = w_in.astype(BF16)
    parts = []
    o = 0
    for s in sizes:
        parts.append(w_in[..., o:o + s])
        o += s
    gq, gk, gv, glr, gg, dq, dk, dv, iq, ik, iw, sq, sk, sv = parts
    pieces = {
        "gla": jnp.concatenate([
            _pad_heads(gq, GLA_HEADS, GLA_DK, GLA_DKP), _pad_heads(gk, GLA_HEADS, GLA_DK, GLA_DKP),
            _pad_heads(gg, GLA_HEADS, GLA_DV, GLA_DVP)], axis=-1),
        "gv": _pad_heads(gv, GLA_HEADS, GLA_DV, GLA_DVP),
        "glr": _pad_last(glr, LANE),
        "iq": iq,
        "dkk": jnp.concatenate([ik, dk], axis=-1),
        "iw": _pad_last(iw, LANE),
        "dv": dv,
        "dq": _pad_last(dq, PAD_W),
        "sq": _pad_last(sq, PAD_W),
        "sk": _pad_last(sk, PAD_W),
        "sv": _pad_last(sv, PAD_W),
    }
    for name, width, _, _, _ in INPROJ_STREAMS:
        assert pieces[name].shape[-1] == width, name
    return jnp.concatenate([pieces[s[0]] for s in INPROJ_STREAMS], axis=-1)


def kernel(x, mem, ln_in_g, ln_in_b, w_in, gla_gate_w2, gla_gate_b, gla_norm_g, dsa_w_uv, w_out,
           ln_mix_g, ln_mix_b, w_mem_q, w_mem_kv, w_mem_o, ln_mem_g, ln_mem_b,
           w_up, b_up, w_down, b_down, ln_ffn_g, ln_ffn_b):
    batch, seq, d = x.shape
    n_mem = mem.shape[1]
    depth = w_in.shape[0]
    x2 = x.reshape(batch * seq, d)
    mem2 = mem.reshape(batch * n_mem, d)
    rows = lambda a: a.reshape(depth, 1, -1)
    gw, dw = GLA_HEADS * GLA_DV, DSA_HEADS * DSA_DH

    w_all = _split_w_in(w_in)
    w2p = jnp.pad(_pad_heads(gla_gate_w2, GLA_HEADS, GLA_DK, GLA_DKP),
                  ((0, 0), (0, LANE - GLA_RANK), (0, 0))).astype(BF16)
    gbp = rows(_pad_heads(gla_gate_b, GLA_HEADS, GLA_DK, GLA_DKP))
    ngp = rows(_pad_last(gla_norm_g, GLA_DVP))
    wuvt = jnp.swapaxes(dsa_w_uv, 2, 3).astype(BF16)
    wo = w_out.astype(BF16)
    wg = jnp.pad(wo[:, :gw].reshape(depth, GLA_HEADS, GLA_DV, d), ((0, 0), (0, 0), (0, GLA_DVP - GLA_DV), (0, 0)))
    wg = wg.reshape(depth, GLA_HEADS * GLA_DVP, d)
    wkv = w_mem_kv.astype(BF16)
    consts = (wg, wo[:, gw:gw + dw], wo[:, gw + dw:], rows(ln_mix_g), rows(ln_mix_b),
              w_mem_q.astype(BF16), w_mem_o.astype(BF16), rows(ln_mem_g), rows(ln_mem_b),
              w_up.astype(BF16), rows(b_up), w_down.astype(BF16), rows(b_down),
              rows(ln_ffn_g), rows(ln_ffn_b))

    h = None
    for l in range(depth):
        if l == 0:
            outs = _inproj(x2, w_all, l, ln=(ln_in_g, ln_in_b))
            h = outs[-1]
            outs = outs[:-1]
        else:
            outs = _inproj(h, w_all, l)
        p_gla, p_gv, p_gvt, p_glr, p_iq, p_dkk, p_iw, p_dvt, p_dq, p_sq, p_sk, p_svt = outs
        o_gla = _gla(p_gla, p_gv, p_gvt, p_glr, w2p, gbp, ngp, l, batch, seq)
        o_dsa = _dsa(p_iq, p_dq, p_iw, p_dkk, p_dvt, wuvt, l, batch, seq)
        o_sb = _sb(p_sq, p_sk, p_svt, batch, seq)
        kv = _memkv(mem2, wkv, l)
        h = _tail(o_gla, o_dsa, o_sb, h, kv, consts, l, batch, seq, n_mem)
    return h.reshape(batch, seq, d)
```

```python
import functools

import jax
import jax.numpy as jnp
import numpy as np
from jax import lax
from jax.experimental import pallas as pl
from jax.experimental.pallas import tpu as pltpu

F32 = jnp.float32
BF16 = jnp.bfloat16
HALF = jnp.bfloat16

DEPTH = 2
LN_EPS = 1e-5
GLA_HEADS, GLA_DK, GLA_DV, GLA_RANK, GLA_TAU, GLA_CHUNK = 4, 48, 96, 16, 16.0, 64
GLA_DKP, GLA_DVP = 64, 128
DSA_HEADS, DSA_DH, DSA_LATENT = 5, 64, 128
IDX_HEADS, IDX_DIM, DSA_TOPK_MAX = 8, 64, 256
SB_HEADS, SB_DH = 5, 64
MEM_HEADS = 4
ALPHA = (2.0 * DEPTH) ** 0.25
ALIBI_SLOPES = tuple(2.0 ** (-8.0 * (i + 1) / DSA_HEADS) for i in range(DSA_HEADS))
ALIBI_TERMS = 3
NEG_INF = float("-inf")
F32_LOWEST = float(jnp.finfo(jnp.float32).min)
INT_MIN = -(2 ** 31)

LANE = 128
SUBLANE = 8
PACKED_SUBLANE = 16
VMEM_LIMIT = 56 * 1024 * 1024


def _dot(a, b):
    return jnp.dot(a, b, preferred_element_type=F32)


def _dot_nt(a, b):
    return lax.dot_general(a, b, (((1,), (1,)), ((), ())), preferred_element_type=F32)


def _ln(x, g, b):
    mu = jnp.mean(x, axis=-1, keepdims=True)
    xc = x - mu
    var = jnp.mean(xc * xc, axis=-1, keepdims=True)
    return xc * lax.rsqrt(var + LN_EPS) * g + b


def _softplus(x):
    return jnp.maximum(x, 0.0) + jnp.log(1.0 + jnp.exp(-jnp.abs(x)))


def _split_dot_left(m, x, terms):
    out = None
    r = x
    for t in range(terms):
        xb = r.astype(BF16)
        d = _dot(m, xb)
        out = d if out is None else out + d
        if t + 1 < terms:
            r = r - xb.astype(F32)
    return out


def _bf16_terms(x, n):
    terms = []
    r = np.float32(x)
    for _ in range(n):
        t = np.float32(np.asarray(r).astype(jnp.bfloat16))
        terms.append(float(t))
        r = np.float32(r - t)
    return tuple(terms)


def _layer_block(a, layer, buffered=False):
    zeros = (0,) * (a.ndim - 1)
    kw = dict(pipeline_mode=pl.Buffered(1)) if buffered else {}
    return pl.BlockSpec((None,) + a.shape[1:], lambda *_: (layer,) + zeros, **kw)


def _params(sem):
    return pltpu.CompilerParams(dimension_semantics=sem, vmem_limit_bytes=VMEM_LIMIT)


PAD_W = 384
INPROJ_CHUNK = 512
INPROJ_STREAMS = (
    ("gla", 2 * GLA_HEADS * GLA_DKP + GLA_HEADS * GLA_DVP, F32, True, False),
    ("gv", GLA_HEADS * GLA_DVP, BF16, True, True),
    ("glr", LANE, BF16, True, False),
    ("iq", IDX_HEADS * IDX_DIM, BF16, True, False),
    ("dkk", IDX_DIM + DSA_DH, BF16, True, False),
    ("iw", LANE, F32, True, False),
    ("dv", DSA_LATENT, BF16, False, True),
    ("dq", PAD_W, BF16, True, False),
    ("sq", PAD_W, BF16, True, False),
    ("sk", PAD_W, BF16, True, False),
    ("sv", PAD_W, BF16, False, True),
)
INPROJ_COLS = sum(s[1] for s in INPROJ_STREAMS)


def _inproj_outputs():
    outs = []
    for si, (_, _, _, plain, tr) in enumerate(INPROJ_STREAMS):
        if plain:
            outs.append((si, False))
        if tr:
            outs.append((si, True))
    return outs


def _inproj_body(*refs, apply_ln):
    if apply_ln:
        x_ref, g_ref, b_ref, w_ref = refs[:4]
        o_refs = refs[4:]
    else:
        x_ref, w_ref = refs[:2]
        o_refs = refs[2:]
    x = x_ref[...]
    if apply_ln:
        x = _ln(x, g_ref[...], b_ref[...])
        o_refs[-1][...] = x
    xb = x.astype(BF16)
    starts = np.cumsum([0] + [s[1] for s in INPROJ_STREAMS])
    outs = _inproj_outputs()
    for c0 in range(0, INPROJ_COLS, INPROJ_CHUNK):
        c1 = c0 + INPROJ_CHUNK
        y = _dot(xb, w_ref[:, c0:c1])
        for o_ref, (si, tr) in zip(o_refs, outs):
            a, b = max(int(starts[si]), c0), min(int(starts[si + 1]), c1)
            if a >= b:
                continue
            part = y[:, a - c0:b - c0]
            lo, hi = a - int(starts[si]), b - int(starts[si])
            if tr:
                o_ref[lo:hi, :] = part.T.astype(o_ref.dtype)
            else:
                o_ref[:, lo:hi] = part.astype(o_ref.dtype)


def _inproj(x, w_all, layer, ln=None, tm=512):
    m, d = x.shape
    apply_ln = ln is not None
    in_specs = [pl.BlockSpec((tm, d), lambda i: (i, 0))]
    args = [x]
    if apply_ln:
        in_specs += [pl.BlockSpec((1, d), lambda i: (0, 0))] * 2
        args += [ln[0].reshape(1, d), ln[1].reshape(1, d)]
    in_specs.append(_layer_block(w_all, layer))
    args.append(w_all)
    out_shape, out_specs = [], []
    for si, tr in _inproj_outputs():
        _, width, dt, _, _ = INPROJ_STREAMS[si]
        if tr:
            out_shape.append(jax.ShapeDtypeStruct((width, m), dt))
            out_specs.append(pl.BlockSpec((width, tm), lambda i: (0, i)))
        else:
            out_shape.append(jax.ShapeDtypeStruct((m, width), dt))
            out_specs.append(pl.BlockSpec((tm, width), lambda i: (i, 0)))
    if apply_ln:
        out_shape.append(jax.ShapeDtypeStruct((m, d), F32))
        out_specs.append(pl.BlockSpec((tm, d), lambda i: (i, 0)))
    return pl.pallas_call(
        functools.partial(_inproj_body, apply_ln=apply_ln),
        grid=(m // tm,),
        in_specs=in_specs,
        out_specs=out_specs,
        out_shape=out_shape,
        compiler_params=_params(("parallel",)),
        name="inproj_ln" if apply_ln else "inproj",
    )(*args)


GLA_KW = GLA_HEADS * GLA_DKP
GLA_VW = GLA_HEADS * GLA_DVP
GLA_Q0, GLA_K0, GLA_G0 = 0, GLA_KW, 2 * GLA_KW
GLA_COLS = GLA_G0 + GLA_VW
GLA_TL = 256
GLA_NC = GLA_TL // GLA_CHUNK


def _gla_body(x_ref, v_ref, vt_ref, r_ref, w2_ref, gb_ref, ng_ref, o_ref, s_ref):
    @pl.when(pl.program_id(1) == 0)
    def _():
        s_ref[...] = jnp.zeros_like(s_ref)

    tl, c, nc = GLA_TL, GLA_CHUNK, GLA_NC
    row = lax.broadcasted_iota(jnp.int32, (tl, tl), 0)
    col = lax.broadcasted_iota(jnp.int32, (tl, tl), 1)
    rch = jnp.floor(row.astype(F32) * (1.0 / c))
    cch = jnp.floor(col.astype(F32) * (1.0 / c))
    causal = (rch == cch) & (col <= row)
    tri = causal.astype(BF16)
    heads = range(GLA_HEADS)
    ksl = [slice(h * GLA_DKP, (h + 1) * GLA_DKP) for h in heads]
    vsl = [slice(h * GLA_DVP, (h + 1) * GLA_DVP) for h in heads]

    xg = _dot(r_ref[...], w2_ref[...]) + gb_ref[...]
    log_a = -_softplus(-xg) / GLA_TAU
    bcum = _split_dot_left(tri, log_a, 3)
    krow = lax.broadcasted_iota(jnp.int32, (tl, GLA_KW), 0)
    blast = bcum[tl - 1:tl, :]
    for j in range(nc - 2, -1, -1):
        blast = jnp.where(krow < (j + 1) * c, bcum[(j + 1) * c - 1:(j + 1) * c, :], blast)
    q = x_ref[:, GLA_Q0:GLA_Q0 + GLA_KW]
    k = x_ref[:, GLA_K0:GLA_K0 + GLA_KW]
    q_t = (q * (GLA_DK ** -0.5) * jnp.exp(bcum)).astype(BF16)
    k_t = (k * jnp.exp(-bcum)).astype(BF16)
    k_end = (k * jnp.exp(blast - bcum)).astype(BF16)
    dec = [jnp.exp(bcum[(j + 1) * c - 1:(j + 1) * c, :]) for j in range(nc)]

    lane_chunk = jnp.floor(lax.broadcasted_iota(jnp.int32, (GLA_DVP, tl), 1).astype(F32) * (1.0 / c))
    o_intra, ds = [], []
    for h in heads:
        sc = jnp.where(causal, _dot_nt(q_t[:, ksl[h]], k_t[:, ksl[h]]), 0.0)
        o_intra.append(_dot(sc.astype(BF16), v_ref[:, vsl[h]]))
        vt = vt_ref[vsl[h], :]
        zero = jnp.zeros_like(vt)
        vt_by_chunk = jnp.concatenate([jnp.where(lane_chunk == float(j), vt, zero) for j in range(nc)], axis=0)
        ds.append(_dot(vt_by_chunk, k_end[:, ksl[h]]))

    ng = ng_ref[...]
    for h in heads:
        s = s_ref[h]
        states = []
        for j in range(nc):
            states.append(s.astype(BF16))
            s = s * dec[j][:, ksl[h]] + ds[h][j * GLA_DVP:(j + 1) * GLA_DVP, :]
        s_ref[h] = s
        inter_all = _dot_nt(q_t[:, ksl[h]], jnp.concatenate(states, axis=0))
        o_inter = jnp.concatenate(
            [inter_all[j * c:(j + 1) * c, j * GLA_DVP:(j + 1) * GLA_DVP] for j in range(nc)], axis=0)
        o = o_intra[h] + o_inter
        ms = jnp.sum(o * o, axis=-1, keepdims=True) * (1.0 / GLA_DV)
        on = o * lax.rsqrt(ms + LN_EPS) * ng
        g = x_ref[:, GLA_G0 + h * GLA_DVP:GLA_G0 + (h + 1) * GLA_DVP]
        o_ref[:, vsl[h]] = (on * (g / (1.0 + jnp.exp(-g)))).astype(o_ref.dtype)


def _gla(x32, v, vt, glr, w2p, gbp, ngp, layer, batch, seq):
    tl = GLA_TL
    nt = seq // tl
    const = lambda a: _layer_block(a, layer)
    return pl.pallas_call(
        _gla_body,
        grid=(batch, nt),
        in_specs=[
            pl.BlockSpec((tl, GLA_COLS), lambda b, t: (b * nt + t, 0)),
            pl.BlockSpec((tl, GLA_VW), lambda b, t: (b * nt + t, 0)),
            pl.BlockSpec((GLA_VW, tl), lambda b, t: (0, b * nt + t)),
            pl.BlockSpec((tl, LANE), lambda b, t: (b * nt + t, 0)),
            const(w2p), const(gbp), const(ngp),
        ],
        out_specs=pl.BlockSpec((tl, GLA_VW), lambda b, t: (b * nt + t, 0)),
        out_shape=jax.ShapeDtypeStruct((batch * seq, GLA_VW), BF16),
        scratch_shapes=[pltpu.VMEM((GLA_HEADS, GLA_DVP, GLA_DKP), F32)],
        compiler_params=_params(("arbitrary", "arbitrary")),
        name="gla",
    )(x32, v, vt, glr, w2p, gbp, ngp)


SB_TQ = 256
SB_CK = 256
SB_W = SB_HEADS * SB_DH
SB_WP = 384


def _sb_body(q_ref, k_ref, vt_ref, o_ref, qs_ref, acc_ref, run_ref):
    tq, ck = SB_TQ, SB_CK
    i = pl.program_id(1)
    c_last = (i + 1) * (tq // ck) - 1
    kloc = lax.broadcasted_iota(jnp.int32, (ck, tq), 0)
    qloc = lax.broadcasted_iota(jnp.int32, (ck, tq), 1)
    qpos = i * tq + qloc
    later = (lax.broadcasted_iota(jnp.int32, (ck, ck), 1) > lax.broadcasted_iota(jnp.int32, (ck, ck), 0)).astype(BF16)
    for h in range(SB_HEADS):
        qs_ref[h] = q_ref[:, h * SB_DH:(h + 1) * SB_DH] * (SB_DH ** -0.5)
    acc_ref[...] = jnp.zeros_like(acc_ref)
    run_ref[...] = jnp.zeros_like(run_ref)

    heads = range(SB_HEADS)
    hsl = [slice(h * SB_DH, (h + 1) * SB_DH) for h in heads]

    def chunk(c, masked):
        k0 = pl.multiple_of(c * ck, ck)
        zs = [_dot_nt(k_ref[pl.ds(k0, ck), hsl[h]], qs_ref[h]) for h in heads]
        runs = run_ref[...]
        sps = [_softplus(z) for z in zs]
        l1s = [-sp for sp in sps]
        if masked:
            valid = k0 + kloc < qpos
            l1s = [jnp.where(valid, l1, 0.0) for l1 in l1s]
        suf = _split_dot_left(later, jnp.concatenate(l1s, axis=1), 2)
        new_runs = []
        for h in heads:
            w = jnp.exp(zs[h] - sps[h] + suf[:, h * tq:(h + 1) * tq] + runs[h:h + 1, :])
            if masked:
                w = jnp.where(valid, w, 0.0)
            acc_ref[hsl[h], :] += _dot(vt_ref[hsl[h], pl.ds(k0, ck)], w.astype(BF16))
            new_runs.append(runs[h:h + 1, :] + jnp.sum(l1s[h], axis=0, keepdims=True))
        new_runs = jnp.concatenate(new_runs, axis=0)
        run_ref[0:SB_HEADS, :] = new_runs
        return (jnp.max(jnp.exp(new_runs + 2.0)) > 0.0).astype(jnp.int32)

    n_diag = tq // ck
    alive = jnp.int32(1)
    for j in range(n_diag):
        alive = chunk(c_last - j, True)

    def cond(st):
        step, alive = st
        return (step <= c_last) & (alive > 0)

    def body(st):
        step, _ = st
        return step + 1, chunk(c_last - step, False)

    lax.while_loop(cond, body, (jnp.int32(n_diag), alive))
    o_ref[...] = acc_ref[...].T[:, :SB_W].astype(o_ref.dtype)


def _sb(q, k, vt, batch, seq):
    tq = SB_TQ
    nq = seq // tq
    return pl.pallas_call(
        _sb_body,
        grid=(batch, nq),
        in_specs=[
            pl.BlockSpec((tq, q.shape[1]), lambda b, i: (b * nq + i, 0)),
            pl.BlockSpec((seq, k.shape[1]), lambda b, i: (b, 0)),
            pl.BlockSpec((vt.shape[0], seq), lambda b, i: (0, b)),
        ],
        out_specs=pl.BlockSpec((tq, SB_W), lambda b, i: (b * nq + i, 0)),
        out_shape=jax.ShapeDtypeStruct((batch * seq, SB_W), BF16),
        scratch_shapes=[
            pltpu.VMEM((SB_HEADS, tq, SB_DH), BF16),
            pltpu.VMEM((SB_WP, tq), F32),
            pltpu.VMEM((SUBLANE, tq), F32),
        ],
        compiler_params=_params(("parallel", "arbitrary")),
        name="sb",
    )(q, k, vt)


DSA_TQ = 256
DSA_CK = 256
DSA_NACC = 4
DSA_CGRP = 8
DSA_W = DSA_HEADS * DSA_DH
DSA_WP = 384
DKK_IK0, DKK_K0 = 0, IDX_DIM
DKK_COLS = IDX_DIM + DSA_DH


def _dsa_body(iq_ref, dq_ref, iw_ref, kk_ref, vt_ref, wuvt_ref, o_ref,
              isc_ref, ihi_ref, cand_ref, thr_ref, nge_ref, iqs_ref, iwt_ref, qs_ref, kaug_ref, sraw_ref,
              m_ref, l_ref, acc_ref, ot_ref,
              *, seq, topk):
    tq, ck = DSA_TQ, DSA_CK
    grp = ck // SUBLANE
    i = pl.program_id(1)
    nch = (i + 1) * (tq // ck)
    kloc = lax.broadcasted_iota(jnp.int32, (ck, tq), 0)
    qloc = lax.broadcasted_iota(jnp.int32, (ck, tq), 1)
    qpos = i * tq + qloc
    rowpos = i * tq + lax.broadcasted_iota(jnp.int32, (1, tq), 1)
    full = rowpos >= topk

    for h in range(IDX_HEADS):
        iqs_ref[h] = iq_ref[:, h * IDX_DIM:(h + 1) * IDX_DIM]
    qlane = lax.broadcasted_iota(jnp.int32, (tq, LANE - DSA_DH), 1)
    for h in range(DSA_HEADS):
        hr = slice(h * tq, (h + 1) * tq)
        qs_ref[hr, 0:DSA_DH] = dq_ref[:, h * DSA_DH:(h + 1) * DSA_DH] * (DSA_DH ** -0.5)
        pieces = jnp.zeros((tq, LANE - DSA_DH), F32)
        for t, piece in enumerate(_bf16_terms(ALIBI_SLOPES[h], ALIBI_TERMS)):
            pieces = jnp.where(qlane == t, piece, pieces)
        qs_ref[hr, DSA_DH:LANE] = pieces.astype(BF16)
    klane = lax.broadcasted_iota(jnp.int32, (ck, LANE - DSA_DH), 1)
    koff = lax.broadcasted_iota(jnp.int32, (ck, LANE - DSA_DH), 0).astype(F32)
    kaug_ref[:, DSA_DH:LANE] = jnp.where(klane < ALIBI_TERMS, koff, 0.0).astype(BF16)
    iwt_ref[...] = (iw_ref[...] * ((IDX_HEADS ** -0.5) * (IDX_DIM ** -0.5))).T[0:IDX_HEADS, :]

    def scores(c):
        k0 = pl.multiple_of(c * ck, ck)
        kaug_ref[:, 0:DSA_DH] = kk_ref[pl.ds(k0, ck), DKK_K0:DKK_K0 + DSA_DH]
        return _dot_nt(kaug_ref[...], qs_ref[...])

    sraw_ref[...] = scores(0)

    def index_chunk(c, carry):
        k0 = pl.multiple_of(c * ck, ck)
        ik = kk_ref[pl.ds(k0, ck), DKK_IK0:DKK_IK0 + IDX_DIM]
        acc = jnp.zeros((ck, tq), F32)
        for h in range(IDX_HEADS):
            rel = jnp.maximum(_dot_nt(ik, iqs_ref[h]), 0.0)
            acc = acc + rel * iwt_ref[h:h + 1, :]
        sc = jnp.where(k0 + kloc <= qpos, acc, NEG_INF)
        isc_ref[pl.ds(k0, ck), :] = sc
        ihi_ref[pl.ds(k0, ck), :] = sc.astype(HALF)
        return carry

    lax.fori_loop(0, nch, index_chunk, 0)

    thr_ref[...] = jnp.full(thr_ref.shape, F32_LOWEST, F32)

    def count(pred, ref=isc_ref, n_chunks=nch):
        def chunk(c, a):
            k0 = pl.multiple_of(c * ck, ck)
            blk = ref[pl.ds(k0, ck), :].reshape(grp, SUBLANE, tq)
            hit = pred(blk, k0).astype(F32).reshape(grp // DSA_NACC, DSA_NACC, SUBLANE, tq)
            return a + jnp.sum(hit, axis=0)
        a = lax.fori_loop(0, n_chunks, chunk, jnp.zeros((DSA_NACC, SUBLANE, tq), F32))
        return jnp.sum(a.reshape(DSA_NACC * SUBLANE, tq), axis=0, keepdims=True)

    def key_to_float(u):
        sk = u ^ INT_MIN
        return pltpu.bitcast(jnp.where(sk >= 0, sk, sk ^ 0x7FFFFFFF), F32)

    def rows8(v):
        return jnp.broadcast_to(v, (SUBLANE, tq))[None]

    def count_high(cand_hi):
        grp16 = ck // PACKED_SUBLANE
        cb = jnp.broadcast_to(cand_hi, (PACKED_SUBLANE, tq))[None]
        one = jnp.ones((), HALF)
        zero = jnp.zeros((), HALF)

        def chunk(c, a):
            k0 = pl.multiple_of(c * ck, ck)
            blk = ihi_ref[pl.ds(k0, ck), :].reshape(grp16, PACKED_SUBLANE, tq)
            hit = jnp.where(blk >= cb, one, zero).reshape(grp16 // DSA_NACC, DSA_NACC, PACKED_SUBLANE, tq)
            for g in range(grp16 // DSA_NACC):
                a = a + hit[g]
            return a
        a = lax.fori_loop(0, nch, chunk, jnp.zeros((DSA_NACC, PACKED_SUBLANE, tq), HALF))
        return jnp.sum(a.astype(F32).reshape(DSA_NACC * PACKED_SUBLANE, tq), axis=0, keepdims=True)

    @pl.when((i + 1) * tq > topk)
    def _search():
        def half_key(u):
            return u | jnp.where((u ^ INT_MIN) < 0, 0xFFFF, 0)

        def coarse_step(it, cur):
            cand = cur | lax.shift_left(jnp.int32(1), 31 - it)
            cnt = count_high(key_to_float(half_key(cand)).astype(HALF))
            return jnp.where(cnt >= topk, cand, cur)

        coarse = lax.fori_loop(0, 16, coarse_step, jnp.zeros((1, tq), jnp.int32))
        base = half_key(coarse) - 0x8000

        span = 1 << 17

        def count_ge(u):
            cf = rows8(key_to_float(u))
            return count(lambda blk, k0: blk >= cf)

        def refine(count_fn, n_base):
            def fine_step(it, st):
                off, n_ge = st
                cand = off | lax.shift_left(jnp.int32(1), 16 - it)
                cnt = count_fn(base + cand)
                ok = cnt >= topk
                return jnp.where(ok, cand, off), jnp.where(ok, cnt, n_ge)

            off, n_ge = lax.fori_loop(0, 17, fine_step, (jnp.zeros((1, tq), jnp.int32), n_base))
            thr_ref[...] = jnp.where(full, key_to_float(base + off), F32_LOWEST)
            nge_ref[...] = n_ge

        bot8 = rows8(key_to_float(base))[0]
        top8 = rows8(key_to_float(base + span))[0]
        ngrp = ck // (DSA_CGRP * SUBLANE)
        crows = 2 * ngrp * SUBLANE

        def extract(c, st):
            a_top, a_lo = st
            k0 = pl.multiple_of(c * ck, ck)
            blk = isc_ref[pl.ds(k0, ck), :].reshape(ngrp, DSA_CGRP, SUBLANE, tq)
            best = jnp.full((ngrp, SUBLANE, tq), NEG_INF, F32)
            second = best
            for r in range(DSA_CGRP):
                x = blk[:, r]
                below = x < top8
                a_top = a_top + jnp.where(below, 0.0, 1.0)
                a_lo = a_lo + jnp.where(x >= bot8, 1.0, 0.0)
                v = jnp.where(below, x, NEG_INF)
                second = jnp.maximum(second, jnp.minimum(best, v))
                best = jnp.maximum(best, v)
            r0 = pl.multiple_of(c * crows, crows)
            cand_ref[pl.ds(r0, crows), :] = jnp.concatenate(
                [best.reshape(ngrp * SUBLANE, tq), second.reshape(ngrp * SUBLANE, tq)], axis=0)
            return a_top, a_lo

        cand_ref[...] = jnp.full(cand_ref.shape, NEG_INF, F32)
        zero_acc = jnp.zeros((ngrp, SUBLANE, tq), F32)
        a_top, a_lo = lax.fori_loop(0, nch, extract, (zero_acc, zero_acc))
        n_top = jnp.sum(a_top.reshape(ngrp * SUBLANE, tq), axis=0, keepdims=True)
        n_lo = jnp.sum(a_lo.reshape(ngrp * SUBLANE, tq), axis=0, keepdims=True)

        def count_small(u):
            cf = rows8(key_to_float(u))
            return n_top + count(lambda blk, k0: blk >= cf, cand_ref, (nch * crows + ck - 1) // ck)

        lost = jnp.max(jnp.where(full, jnp.abs(n_lo - count_small(base)), 0.0))

        @pl.when(lost == 0.0)
        def _small():
            refine(count_small, n_lo)

        @pl.when(lost > 0.0)
        def _full():
            refine(count_ge, n_lo)

        thr = thr_ref[...]
        n_ge = nge_ref[...]
        excess = jnp.max(jnp.where(full, n_ge - topk, 0.0))

        @pl.when(excess > 0.0)
        def _ties():
            thr8 = rows8(thr)
            need = topk - count(lambda blk, k0: blk > thr8)
            nbits = (seq - 1).bit_length()
            kpos3 = lax.broadcasted_iota(jnp.int32, (grp, SUBLANE, tq), 0) * SUBLANE + \
                lax.broadcasted_iota(jnp.int32, (grp, SUBLANE, tq), 1)

            def pos_step(it, p):
                cand = p | lax.shift_left(jnp.int32(1), nbits - 1 - it)
                cand8 = rows8(cand)
                cnt = count(lambda blk, k0: (blk == thr8) & (k0 + kpos3 < cand8))
                return jnp.where(cnt < need, cand, p)

            last = lax.fori_loop(0, nbits, pos_step, jnp.zeros((1, tq), jnp.int32))

            def drop(c, carry):
                k0 = pl.multiple_of(c * ck, ck)
                blk = isc_ref[pl.ds(k0, ck), :]
                kill = (blk == thr) & (k0 + kloc > last) & full
                isc_ref[pl.ds(k0, ck), :] = jnp.where(kill, NEG_INF, blk)
                return carry

            lax.fori_loop(0, nch, drop, 0)

    m_ref[...] = jnp.full(m_ref.shape, NEG_INF, F32)
    l_ref[...] = jnp.zeros_like(l_ref)
    acc_ref[...] = jnp.zeros_like(acc_ref)
    thr = thr_ref[...]
    slope_row = jnp.concatenate([jnp.full((1, tq), sl, F32) for sl in ALIBI_SLOPES], axis=1)

    def attend(c, carry):
        k0 = pl.multiple_of(c * ck, ck)
        s5 = sraw_ref[...]
        s_next = scores(jnp.minimum(c + 1, nch - 1))
        sel = isc_ref[pl.ds(k0, ck), :] >= thr
        s5 = jnp.concatenate([jnp.where(sel, s5[:, h * tq:(h + 1) * tq], NEG_INF)
                              for h in range(DSA_HEADS)], axis=1)
        cvec = slope_row * k0.astype(F32)
        m_old = m_ref[...]
        m_new = jnp.maximum(m_old, jnp.max(s5, axis=0, keepdims=True) + cvec)
        m_safe = jnp.where(m_new == NEG_INF, 0.0, m_new)
        alpha = jnp.exp(m_old - m_safe)
        p = jnp.exp(s5 - (m_safe - cvec))
        l_ref[...] = alpha * l_ref[...] + jnp.sum(p, axis=0, keepdims=True)
        acc_ref[...] = alpha * acc_ref[...] + _dot(vt_ref[:, pl.ds(k0, ck)], p.astype(BF16))
        m_ref[...] = m_new
        sraw_ref[...] = s_next
        return carry

    lax.fori_loop(0, nch, attend, 0)

    ot_ref[DSA_W:, :] = jnp.zeros((DSA_WP - DSA_W, tq), F32)
    for h in range(DSA_HEADS):
        hs = slice(h * tq, (h + 1) * tq)
        o_lat = acc_ref[:, hs] / l_ref[:, hs]
        ot_ref[h * DSA_DH:(h + 1) * DSA_DH, :] = _dot(wuvt_ref[h], o_lat.astype(BF16))
    o_ref[...] = ot_ref[...].T[:, :DSA_W].astype(o_ref.dtype)


def _dsa(iq, dq, iw, dkk, dvt, wuvt, layer, batch, seq):
    tq = DSA_TQ
    nq = seq // tq
    topk = min(DSA_TOPK_MAX, seq // 4)
    return pl.pallas_call(
        functools.partial(_dsa_body, seq=seq, topk=topk),
        grid=(batch, nq),
        in_specs=[
            pl.BlockSpec((tq, iq.shape[1]), lambda b, i: (b * nq + i, 0)),
            pl.BlockSpec((tq, dq.shape[1]), lambda b, i: (b * nq + i, 0)),
            pl.BlockSpec((tq, iw.shape[1]), lambda b, i: (b * nq + i, 0)),
            pl.BlockSpec((seq, DKK_COLS), lambda b, i: (b, 0)),
            pl.BlockSpec((DSA_LATENT, seq), lambda b, i: (0, b)),
            _layer_block(wuvt, layer),
        ],
        out_specs=pl.BlockSpec((tq, DSA_W), lambda b, i: (b * nq + i, 0)),
        out_shape=jax.ShapeDtypeStruct((batch * seq, DSA_W), BF16),
        scratch_shapes=[
            pltpu.VMEM((seq, tq), F32),
            pltpu.VMEM((seq, tq), HALF),
            pltpu.VMEM((pl.cdiv(2 * seq // DSA_CGRP, DSA_CK) * DSA_CK, tq), F32),
            pltpu.VMEM((1, tq), F32),
            pltpu.VMEM((1, tq), F32),
            pltpu.VMEM((IDX_HEADS, tq, IDX_DIM), BF16),
            pltpu.VMEM((IDX_HEADS, tq), F32),
            pltpu.VMEM((DSA_HEADS * tq, LANE), BF16),
            pltpu.VMEM((DSA_CK, LANE), BF16),
            pltpu.VMEM((DSA_CK, DSA_HEADS * tq), F32),
            pltpu.VMEM((1, DSA_HEADS * tq), F32),
            pltpu.VMEM((1, DSA_HEADS * tq), F32),
            pltpu.VMEM((DSA_LATENT, DSA_HEADS * tq), F32),
            pltpu.VMEM((DSA_WP, tq), F32),
        ],
        compiler_params=_params(("parallel", "arbitrary")),
        name="dsa",
    )(iq, dq, iw, dkk, dvt, wuvt)


def _outproj_body(og_ref, od_ref, os_ref, h_ref, wg_ref, wd_ref, ws_ref, g_ref, b_ref, o_ref):
    f = _dot(og_ref[...], wg_ref[...]) + _dot(od_ref[...], wd_ref[...]) + _dot(os_ref[...], ws_ref[...])
    o_ref[...] = _ln(ALPHA * h_ref[...] + f, g_ref[...], b_ref[...])


def _memkv_body(mem_ref, w_ref, kv_ref):
    kv_ref[...] = _dot(mem_ref[...].astype(BF16), w_ref[...]).astype(kv_ref.dtype)


def _memkv(mem2d, wkv, layer, tm=512):
    m, d = mem2d.shape
    n = wkv.shape[-1]
    return pl.pallas_call(
        _memkv_body,
        grid=(m // tm,),
        in_specs=[pl.BlockSpec((tm, d), lambda i: (i, 0)), _layer_block(wkv, layer)],
        out_specs=pl.BlockSpec((tm, n), lambda i: (i, 0)),
        out_shape=jax.ShapeDtypeStruct((m, n), BF16),
        compiler_params=_params(("parallel",)),
        name="mem_kv",
    )(mem2d, wkv)


MEM_ROWS = 512


def _memattn_body(h_ref, kv_ref, wq_ref, wo_ref, g_ref, b_ref, o_ref, ctx_ref):
    tm, d = h_ref.shape
    dh = d // MEM_HEADS
    groups = [slice(r0, r0 + MEM_ROWS) for r0 in range(0, tm, MEM_ROWS)]
    qs = [_dot(h_ref[rs, :].astype(BF16), wq_ref[...]).astype(BF16) for rs in groups]

    def attend(rs, q):
        def scores(hd):
            cs = slice(hd * dh, (hd + 1) * dh)
            return _dot_nt(q[:, cs], kv_ref[:, cs]) * (dh ** -0.5)

        s = scores(0)
        ctx_prev = None
        for hd in range(MEM_HEADS):
            s_next = scores(hd + 1) if hd + 1 < MEM_HEADS else None
            s = s - jnp.max(s, axis=-1, keepdims=True)
            p = jnp.exp(s)
            p = p / jnp.sum(p, axis=-1, keepdims=True)
            ctx = _dot(p.astype(BF16), kv_ref[:, d + hd * dh:d + (hd + 1) * dh])
            if ctx_prev is not None:
                ctx_ref[rs, (hd - 1) * dh:hd * dh] = ctx_prev.astype(BF16)
            ctx_prev = ctx
            s = s_next
        ctx_ref[rs, (MEM_HEADS - 1) * dh:] = ctx_prev.astype(BF16)
        return _dot(ctx_ref[rs, :], wo_ref[...])

    fs = [attend(rs, q) for rs, q in zip(groups, qs)]
    for rs, f in zip(groups, fs):
        o_ref[rs, :] = _ln(ALPHA * h_ref[rs, :] + f, g_ref[...], b_ref[...])


MLP_FC = 512


def _mlp_body(h_ref, wu_ref, bu_ref, wd_ref, bd_ref, g_ref, b_ref, o_ref):
    h = h_ref[...]
    hb = h.astype(BF16)
    f = None
    for c0 in range(0, wu_ref.shape[1], MLP_FC):
        cs = slice(c0, c0 + MLP_FC)
        u = jnp.maximum(_dot(hb, wu_ref[:, cs]) + bu_ref[:, cs], 0.0)
        d = _dot((u * u).astype(BF16), wd_ref[cs, :])
        f = d if f is None else f + d
    o_ref[...] = _ln(ALPHA * h + (f + bd_ref[...]), g_ref[...], b_ref[...])


def _tail_body(og_ref, od_ref, os_ref, h_ref, kv_ref, wg_ref, wd_ref, ws_ref, g1_ref, b1_ref,
               wq_ref, wo_ref, g2_ref, b2_ref, wu_ref, bu_ref, wdn_ref, bdn_ref, g3_ref, b3_ref,
               o_ref, hs_ref, ctx_ref):
    _outproj_body(og_ref, od_ref, os_ref, h_ref, wg_ref, wd_ref, ws_ref, g1_ref, b1_ref, hs_ref)
    _memattn_body(hs_ref, kv_ref, wq_ref, wo_ref, g2_ref, b2_ref, hs_ref, ctx_ref)
    _mlp_body(hs_ref, wu_ref, bu_ref, wdn_ref, bdn_ref, g3_ref, b3_ref, o_ref)


def _tail(og, od, os_, h, kv, consts, layer, batch, seq, n_mem, tm=512):
    m, d = h.shape
    nt = seq // tm
    row = lambda a: pl.BlockSpec((tm, a.shape[1]), lambda bb, t: (bb * nt + t, 0))
    const = lambda a: _layer_block(a, layer, buffered=True)
    return pl.pallas_call(
        _tail_body,
        grid=(batch, nt),
        in_specs=[row(og), row(od), row(os_), row(h),
                  pl.BlockSpec((n_mem, 2 * d), lambda bb, t: (bb, 0))] + [const(a) for a in consts],
        out_specs=pl.BlockSpec((tm, d), lambda bb, t: (bb * nt + t, 0)),
        out_shape=jax.ShapeDtypeStruct((m, d), F32),
        scratch_shapes=[pltpu.VMEM((tm, d), F32), pltpu.VMEM((tm, d), BF16)],
        compiler_params=_params(("parallel", "parallel")),
        name="tail",
    )(og, od, os_, h, kv, *consts)


def _pad_heads(w, heads, width, padded):
    lead = w.shape[:-1]
    w = w.reshape(lead + (heads, width))
    w = jnp.pad(w, [(0, 0)] * len(lead) + [(0, 0), (0, padded - width)])
    return w.reshape(lead + (heads * padded,))


def _pad_last(w, n):
    return jnp.pad(w, [(0, 0)] * (w.ndim - 1) + [(0, n - w.shape[-1])])


def _split_w_in(w_in):
    sizes = (GLA_HEADS * GLA_DK, GLA_HEADS * GLA_DK, GLA_HEADS * GLA_DV, GLA_RANK, GLA_HEADS * GLA_DV,
             DSA_HEADS * DSA_DH, DSA_DH, DSA_LATENT, IDX_HEADS * IDX_DIM, IDX_DIM, IDX_HEADS,
             SB_HEADS * SB_DH, SB_HEADS * SB_DH, SB_HEADS * SB_DH)
    w_in = w_in.astype(BF16)
    parts = []
    o = 0
    for s in sizes:
        parts.append(w_in[..., o:o + s])
        o += s
    gq, gk, gv, glr, gg, dq, dk, dv, iq, ik, iw, sq, sk, sv = parts
    pieces = {
        "gla": jnp.concatenate([
            _pad_heads(gq, GLA_HEADS, GLA_DK, GLA_DKP), _pad_heads(gk, GLA_HEADS, GLA_DK, GLA_DKP),
            _pad_heads(gg, GLA_HEADS, GLA_DV, GLA_DVP)], axis=-1),
        "gv": _pad_heads(gv, GLA_HEADS, GLA_DV, GLA_DVP),
        "glr": _pad_last(glr, LANE),
        "iq": iq,
        "dkk": jnp.concatenate([ik, dk], axis=-1),
        "iw": _pad_last(iw, LANE),
        "dv": dv,
        "dq": _pad_last(dq, PAD_W),
        "sq": _pad_last(sq, PAD_W),
        "sk": _pad_last(sk, PAD_W),
        "sv": _pad_last(sv, PAD_W),
    }
    for name, width, _, _, _ in INPROJ_STREAMS:
        assert pieces[name].shape[-1] == width, name
    return jnp.concatenate([pieces[s[0]] for s in INPROJ_STREAMS], axis=-1)


def kernel(x, mem, ln_in_g, ln_in_b, w_in, gla_gate_w2, gla_gate_b, gla_norm_g, dsa_w_uv, w_out,
           ln_mix_g, ln_mix_b, w_mem_q, w_mem_kv, w_mem_o, ln_mem_g, ln_mem_b,
           w_up, b_up, w_down, b_down, ln_ffn_g, ln_ffn_b):
    batch, seq, d = x.shape
    n_mem = mem.shape[1]
    depth = w_in.shape[0]
    x2 = x.reshape(batch * seq, d)
    mem2 = mem.reshape(batch * n_mem, d)
    rows = lambda a: a.reshape(depth, 1, -1)
    gw, dw = GLA_HEADS * GLA_DV, DSA_HEADS * DSA_DH

    w_all = _split_w_in(w_in)
    w2p = jnp.pad(_pad_heads(gla_gate_w2, GLA_HEADS, GLA_DK, GLA_DKP),
                  ((0, 0), (0, LANE - GLA_RANK), (0, 0))).astype(BF16)
    gbp = rows(_pad_heads(gla_gate_b, GLA_HEADS, GLA_DK, GLA_DKP))
    ngp = rows(_pad_last(gla_norm_g, GLA_DVP))
    wuvt = jnp.swapaxes(dsa_w_uv, 2, 3).astype(BF16)
    wo = w_out.astype(BF16)
    wg = jnp.pad(wo[:, :gw].reshape(depth, GLA_HEADS, GLA_DV, d), ((0, 0), (0, 0), (0, GLA_DVP - GLA_DV), (0, 0)))
    wg = wg.reshape(depth, GLA_HEADS * GLA_DVP, d)
    wkv = w_mem_kv.astype(BF16)
    consts = (wg, wo[:, gw:gw + dw], wo[:, gw + dw:], rows(ln_mix_g), rows(ln_mix_b),
              w_mem_q.astype(BF16), w_mem_o.astype(BF16), rows(ln_mem_g), rows(ln_mem_b),
              w_up.astype(BF16), rows(b_up), w_down.astype(BF16), rows(b_down),
              rows(ln_ffn_g), rows(ln_ffn_b))

    h = None
    for l in range(depth):
        if l == 0:
            outs = _inproj(x2, w_all, l, ln=(ln_in_g, ln_in_b))
            h = outs[-1]
            outs = outs[:-1]
        else:
            outs = _inproj(h, w_all, l)
        p_gla, p_gv, p_gvt, p_glr, p_iq, p_dkk, p_iw, p_dvt, p_dq, p_sq, p_sk, p_svt = outs
        o_gla = _gla(p_gla, p_gv, p_gvt, p_glr, w2p, gbp, ngp, l, batch, seq)
        o_dsa = _dsa(p_iq, p_dq, p_iw, p_dkk, p_dvt, wuvt, l, batch, seq)
        o_sb = _sb(p_sq, p_sk, p_svt, batch, seq)
        kv = _memkv(mem2, wkv, l)
        h = _tail(o_gla, o_dsa, o_sb, h, kv, consts, l, batch, seq, n_mem)
    return h.reshape(batch, seq, d)
```

```python
import functools

import jax
import jax.numpy as jnp
import numpy as np
from jax import lax
from jax.experimental import pallas as pl
from jax.experimental.pallas import tpu as pltpu

F32 = jnp.float32
BF16 = jnp.bfloat16
HALF = jnp.bfloat16

DEPTH = 2
LN_EPS = 1e-5
GLA_HEADS, GLA_DK, GLA_DV, GLA_RANK, GLA_TAU, GLA_CHUNK = 4, 48, 96, 16, 16.0, 64
GLA_DKP, GLA_DVP = 64, 128
DSA_HEADS, DSA_DH, DSA_LATENT = 5, 64, 128
IDX_HEADS, IDX_DIM, DSA_TOPK_MAX = 8, 64, 256
SB_HEADS, SB_DH = 5, 64
MEM_HEADS = 4
ALPHA = (2.0 * DEPTH) ** 0.25
ALIBI_SLOPES = tuple(2.0 ** (-8.0 * (i + 1) / DSA_HEADS) for i in range(DSA_HEADS))
ALIBI_TERMS = 3
NEG_INF = float("-inf")
F32_LOWEST = float(jnp.finfo(jnp.float32).min)
INT_MIN = -(2 ** 31)

LANE = 128
SUBLANE = 8
PACKED_SUBLANE = 16
VMEM_LIMIT = 56 * 1024 * 1024


def _dot(a, b):
    return jnp.dot(a, b, preferred_element_type=F32)


def _dot_nt(a, b):
    return lax.dot_general(a, b, (((1,), (1,)), ((), ())), preferred_element_type=F32)


def _ln(x, g, b):
    mu = jnp.mean(x, axis=-1, keepdims=True)
    xc = x - mu
    var = jnp.mean(xc * xc, axis=-1, keepdims=True)
    return xc * lax.rsqrt(var + LN_EPS) * g + b


def _softplus(x):
    return jnp.maximum(x, 0.0) + jnp.log(1.0 + jnp.exp(-jnp.abs(x)))


def _split_dot_left(m, x, terms):
    out = None
    r = x
    for t in range(terms):
        xb = r.astype(BF16)
        d = _dot(m, xb)
        out = d if out is None else out + d
        if t + 1 < terms:
            r = r - xb.astype(F32)
    return out


def _bf16_terms(x, n):
    terms = []
    r = np.float32(x)
    for _ in range(n):
        t = np.float32(np.asarray(r).astype(jnp.bfloat16))
        terms.append(float(t))
        r = np.float32(r - t)
    return tuple(terms)


def _layer_block(a, layer, buffered=False):
    zeros = (0,) * (a.ndim - 1)
    kw = dict(pipeline_mode=pl.Buffered(1)) if buffered else {}
    return pl.BlockSpec((None,) + a.shape[1:], lambda *_: (layer,) + zeros, **kw)


def _params(sem):
    return pltpu.CompilerParams(dimension_semantics=sem, vmem_limit_bytes=VMEM_LIMIT)


PAD_W = 384
INPROJ_CHUNK = 512
INPROJ_STREAMS = (
    ("gla", 2 * GLA_HEADS * GLA_DKP + GLA_HEADS * GLA_DVP, F32, True, False),
    ("gv", GLA_HEADS * GLA_DVP, BF16, True, True),
    ("glr", LANE, BF16, True, False),
    ("iq", IDX_HEADS * IDX_DIM, BF16, True, False),
    ("dkk", IDX_DIM + DSA_DH, BF16, True, False),
    ("iw", LANE, F32, True, False),
    ("dv", DSA_LATENT, BF16, False, True),
    ("dq", PAD_W, BF16, True, False),
    ("sq", PAD_W, BF16, True, False),
    ("sk", PAD_W, BF16, True, False),
    ("sv", PAD_W, BF16, False, True),
)
INPROJ_COLS = sum(s[1] for s in INPROJ_STREAMS)


def _inproj_outputs():
    outs = []
    for si, (_, _, _, plain, tr) in enumerate(INPROJ_STREAMS):
        if plain:
            outs.append((si, False))
        if tr:
            outs.append((si, True))
    return outs


def _inproj_body(*refs, apply_ln):
    if apply_ln:
        x_ref, g_ref, b_ref, w_ref = refs[:4]
        o_refs = refs[4:]
    else:
        x_ref, w_ref = refs[:2]
        o_refs = refs[2:]
    x = x_ref[...]
    if apply_ln:
        x = _ln(x, g_ref[...], b_ref[...])
        o_refs[-1][...] = x
    xb = x.astype(BF16)
    starts = np.cumsum([0] + [s[1] for s in INPROJ_STREAMS])
    outs = _inproj_outputs()
    for c0 in range(0, INPROJ_COLS, INPROJ_CHUNK):
        c1 = c0 + INPROJ_CHUNK
        y = _dot(xb, w_ref[:, c0:c1])
        for o_ref, (si, tr) in zip(o_refs, outs):
            a, b = max(int(starts[si]), c0), min(int(starts[si + 1]), c1)
            if a >= b:
                continue
            part = y[:, a - c0:b - c0]
            lo, hi = a - int(starts[si]), b - int(starts[si])
            if tr:
                o_ref[lo:hi, :] = part.T.astype(o_ref.dtype)
            else:
                o_ref[:, lo:hi] = part.astype(o_ref.dtype)


def _inproj(x, w_all, layer, ln=None, tm=512):
    m, d = x.shape
    apply_ln = ln is not None
    in_specs = [pl.BlockSpec((tm, d), lambda i: (i, 0))]
    args = [x]
    if apply_ln:
        in_specs += [pl.BlockSpec((1, d), lambda i: (0, 0))] * 2
        args += [ln[0].reshape(1, d), ln[1].reshape(1, d)]
    in_specs.append(_layer_block(w_all, layer))
    args.append(w_all)
    out_shape, out_specs = [], []
    for si, tr in _inproj_outputs():
        _, width, dt, _, _ = INPROJ_STREAMS[si]
        if tr:
            out_shape.append(jax.ShapeDtypeStruct((width, m), dt))
            out_specs.append(pl.BlockSpec((width, tm), lambda i: (0, i)))
        else:
            out_shape.append(jax.ShapeDtypeStruct((m, width), dt))
            out_specs.append(pl.BlockSpec((tm, width), lambda i: (i, 0)))
    if apply_ln:
        out_shape.append(jax.ShapeDtypeStruct((m, d), F32))
        out_specs.append(pl.BlockSpec((tm, d), lambda i: (i, 0)))
    return pl.pallas_call(
        functools.partial(_inproj_body, apply_ln=apply_ln),
        grid=(m // tm,),
        in_specs=in_specs,
        out_specs=out_specs,
        out_shape=out_shape,
        compiler_params=_params(("parallel",)),
        name="inproj_ln" if apply_ln else "inproj",
    )(*args)


GLA_KW = GLA_HEADS * GLA_DKP
GLA_VW = GLA_HEADS * GLA_DVP
GLA_Q0, GLA_K0, GLA_G0 = 0, GLA_KW, 2 * GLA_KW
GLA_COLS = GLA_G0 + GLA_VW
GLA_TL = 256
GLA_NC = GLA_TL // GLA_CHUNK


def _gla_body(x_ref, v_ref, vt_ref, r_ref, w2_ref, gb_ref, ng_ref, o_ref, s_ref):
    @pl.when(pl.program_id(1) == 0)
    def _():
        s_ref[...] = jnp.zeros_like(s_ref)

    tl, c, nc = GLA_TL, GLA_CHUNK, GLA_NC
    row = lax.broadcasted_iota(jnp.int32, (tl, tl), 0)
    col = lax.broadcasted_iota(jnp.int32, (tl, tl), 1)
    rch = jnp.floor(row.astype(F32) * (1.0 / c))
    cch = jnp.floor(col.astype(F32) * (1.0 / c))
    causal = (rch == cch) & (col <= row)
    tri = causal.astype(BF16)
    heads = range(GLA_HEADS)
    ksl = [slice(h * GLA_DKP, (h + 1) * GLA_DKP) for h in heads]
    vsl = [slice(h * GLA_DVP, (h + 1) * GLA_DVP) for h in heads]

    xg = _dot(r_ref[...], w2_ref[...]) + gb_ref[...]
    log_a = -_softplus(-xg) / GLA_TAU
    bcum = _split_dot_left(tri, log_a, 3)
    krow = lax.broadcasted_iota(jnp.int32, (tl, GLA_KW), 0)
    blast = bcum[tl - 1:tl, :]
    for j in range(nc - 2, -1, -1):
        blast = jnp.where(krow < (j + 1) * c, bcum[(j + 1) * c - 1:(j + 1) * c, :], blast)
    q = x_ref[:, GLA_Q0:GLA_Q0 + GLA_KW]
    k = x_ref[:, GLA_K0:GLA_K0 + GLA_KW]
    q_t = (q * (GLA_DK ** -0.5) * jnp.exp(bcum)).astype(BF16)
    k_t = (k * jnp.exp(-bcum)).astype(BF16)
    k_end = (k * jnp.exp(blast - bcum)).astype(BF16)
    dec = [jnp.exp(bcum[(j + 1) * c - 1:(j + 1) * c, :]) for j in range(nc)]

    lane_chunk = jnp.floor(lax.broadcasted_iota(jnp.int32, (GLA_DVP, tl), 1).astype(F32) * (1.0 / c))
    o_intra, ds = [], []
    for h in heads:
        sc = jnp.where(causal, _dot_nt(q_t[:, ksl[h]], k_t[:, ksl[h]]), 0.0)
        o_intra.append(_dot(sc.astype(BF16), v_ref[:, vsl[h]]))
        vt = vt_ref[vsl[h], :]
        zero = jnp.zeros_like(vt)
        vt_by_chunk = jnp.concatenate([jnp.where(lane_chunk == float(j), vt, zero) for j in range(nc)], axis=0)
        ds.append(_dot(vt_by_chunk, k_end[:, ksl[h]]))

    ng = ng_ref[...]
    for h in heads:
        s = s_ref[h]
        states = []
        for j in range(nc):
            states.append(s.astype(BF16))
            s = s * dec[j][:, ksl[h]] + ds[h][j * GLA_DVP:(j + 1) * GLA_DVP, :]
        s_ref[h] = s
        inter_all = _dot_nt(q_t[:, ksl[h]], jnp.concatenate(states, axis=0))
        o_inter = jnp.concatenate(
            [inter_all[j * c:(j + 1) * c, j * GLA_DVP:(j + 1) * GLA_DVP] for j in range(nc)], axis=0)
        o = o_intra[h] + o_inter
        ms = jnp.sum(o * o, axis=-1, keepdims=True) * (1.0 / GLA_DV)
        on = o * lax.rsqrt(ms + LN_EPS) * ng
        g = x_ref[:, GLA_G0 + h * GLA_DVP:GLA_G0 + (h + 1) * GLA_DVP]
        o_ref[:, vsl[h]] = (on * (g / (1.0 + jnp.exp(-g)))).astype(o_ref.dtype)


def _gla(x32, v, vt, glr, w2p, gbp, ngp, layer, batch, seq):
    tl = GLA_TL
    nt = seq // tl
    const = lambda a: _layer_block(a, layer)
    return pl.pallas_call(
        _gla_body,
        grid=(batch, nt),
        in_specs=[
            pl.BlockSpec((tl, GLA_COLS), lambda b, t: (b * nt + t, 0)),
            pl.BlockSpec((tl, GLA_VW), lambda b, t: (b * nt + t, 0)),
            pl.BlockSpec((GLA_VW, tl), lambda b, t: (0, b * nt + t)),
            pl.BlockSpec((tl, LANE), lambda b, t: (b * nt + t, 0)),
            const(w2p), const(gbp), const(ngp),
        ],
        out_specs=pl.BlockSpec((tl, GLA_VW), lambda b, t: (b * nt + t, 0)),
        out_shape=jax.ShapeDtypeStruct((batch * seq, GLA_VW), BF16),
        scratch_shapes=[pltpu.VMEM((GLA_HEADS, GLA_DVP, GLA_DKP), F32)],
        compiler_params=_params(("arbitrary", "arbitrary")),
        name="gla",
    )(x32, v, vt, glr, w2p, gbp, ngp)


SB_TQ = 256
SB_CK = 256
SB_W = SB_HEADS * SB_DH
SB_WP = 384


def _sb_body(q_ref, k_ref, vt_ref, o_ref, qs_ref, acc_ref, run_ref):
    tq, ck = SB_TQ, SB_CK
    i = pl.program_id(1)
    c_last = (i + 1) * (tq // ck) - 1
    kloc = lax.broadcasted_iota(jnp.int32, (ck, tq), 0)
    qloc = lax.broadcasted_iota(jnp.int32, (ck, tq), 1)
    qpos = i * tq + qloc
    later = (lax.broadcasted_iota(jnp.int32, (ck, ck), 1) > lax.broadcasted_iota(jnp.int32, (ck, ck), 0)).astype(BF16)
    for h in range(SB_HEADS):
        qs_ref[h] = q_ref[:, h * SB_DH:(h + 1) * SB_DH] * (SB_DH ** -0.5)
    acc_ref[...] = jnp.zeros_like(acc_ref)
    run_ref[...] = jnp.zeros_like(run_ref)

    heads = range(SB_HEADS)
    hsl = [slice(h * SB_DH, (h + 1) * SB_DH) for h in heads]

    def chunk(c, masked):
        k0 = pl.multiple_of(c * ck, ck)
        zs = [_dot_nt(k_ref[pl.ds(k0, ck), hsl[h]], qs_ref[h]) for h in heads]
        runs = run_ref[...]
        sps = [_softplus(z) for z in zs]
        l1s = [-sp for sp in sps]
        if masked:
            valid = k0 + kloc < qpos
            l1s = [jnp.where(valid, l1, 0.0) for l1 in l1s]
        suf = _split_dot_left(later, jnp.concatenate(l1s, axis=1), 2)
        new_runs = []
        for h in heads:
            w = jnp.exp(zs[h] - sps[h] + suf[:, h * tq:(h + 1) * tq] + runs[h:h + 1, :])
            if masked:
                w = jnp.where(valid, w, 0.0)
            acc_ref[hsl[h], :] += _dot(vt_ref[hsl[h], pl.ds(k0, ck)], w.astype(BF16))
            new_runs.append(runs[h:h + 1, :] + jnp.sum(l1s[h], axis=0, keepdims=True))
        new_runs = jnp.concatenate(new_runs, axis=0)
        run_ref[0:SB_HEADS, :] = new_runs
        return (jnp.max(jnp.exp(new_runs + 2.0)) > 0.0).astype(jnp.int32)

    n_diag = tq // ck
    alive = jnp.int32(1)
    for j in range(n_diag):
        alive = chunk(c_last - j, True)

    def cond(st):
        step, alive = st
        return (step <= c_last) & (alive > 0)

    def body(st):
        step, _ = st
        return step + 1, chunk(c_last - step, False)

    lax.while_loop(cond, body, (jnp.int32(n_diag), alive))
    o_ref[...] = acc_ref[...].T[:, :SB_W].astype(o_ref.dtype)


def _sb(q, k, vt, batch, seq):
    tq = SB_TQ
    nq = seq // tq
    return pl.pallas_call(
        _sb_body,
        grid=(batch, nq),
        in_specs=[
            pl.BlockSpec((tq, q.shape[1]), lambda b, i: (b * nq + i, 0)),
            pl.BlockSpec((seq, k.shape[1]), lambda b, i: (b, 0)),
            pl.BlockSpec((vt.shape[0], seq), lambda b, i: (0, b)),
        ],
        out_specs=pl.BlockSpec((tq, SB_W), lambda b, i: (b * nq + i, 0)),
        out_shape=jax.ShapeDtypeStruct((batch * seq, SB_W), BF16),
        scratch_shapes=[
            pltpu.VMEM((SB_HEADS, tq, SB_DH), BF16),
            pltpu.VMEM((SB_WP, tq), F32),
            pltpu.VMEM((SUBLANE, tq), F32),
        ],
        compiler_params=_params(("parallel", "arbitrary")),
        name="sb",
    )(q, k, vt)


DSA_TQ = 256
DSA_CK = 256
DSA_NACC = 4
DSA_CGRP = 8
DSA_W = DSA_HEADS * DSA_DH
DSA_WP = 384
DKK_IK0, DKK_K0 = 0, IDX_DIM
DKK_COLS = IDX_DIM + DSA_DH


def _dsa_body(iq_ref, dq_ref, iw_ref, kk_ref, vt_ref, wuvt_ref, o_ref,
              isc_ref, ihi_ref, cand_ref, thr_ref, nge_ref, iqs_ref, iwt_ref, qs_ref, kaug_ref,
              m_ref, l_ref, acc_ref, ot_ref,
              *, seq, topk):
    tq, ck = DSA_TQ, DSA_CK
    grp = ck // SUBLANE
    i = pl.program_id(1)
    nch = (i + 1) * (tq // ck)
    kloc = lax.broadcasted_iota(jnp.int32, (ck, tq), 0)
    qloc = lax.broadcasted_iota(jnp.int32, (ck, tq), 1)
    qpos = i * tq + qloc
    rowpos = i * tq + lax.broadcasted_iota(jnp.int32, (1, tq), 1)
    full = rowpos >= topk

    for h in range(IDX_HEADS):
        iqs_ref[h] = iq_ref[:, h * IDX_DIM:(h + 1) * IDX_DIM]
    qlane = lax.broadcasted_iota(jnp.int32, (tq, LANE - DSA_DH), 1)
    for h in range(DSA_HEADS):
        hr = slice(h * tq, (h + 1) * tq)
        qs_ref[hr, 0:DSA_DH] = dq_ref[:, h * DSA_DH:(h + 1) * DSA_DH] * (DSA_DH ** -0.5)
        pieces = jnp.zeros((tq, LANE - DSA_DH), F32)
        for t, piece in enumerate(_bf16_terms(ALIBI_SLOPES[h], ALIBI_TERMS)):
            pieces = jnp.where(qlane == t, piece, pieces)
        qs_ref[hr, DSA_DH:LANE] = pieces.astype(BF16)
    klane = lax.broadcasted_iota(jnp.int32, (ck, LANE - DSA_DH), 1)
    koff = lax.broadcasted_iota(jnp.int32, (ck, LANE - DSA_DH), 0).astype(F32)
    kaug_ref[:, DSA_DH:LANE] = jnp.where(klane < ALIBI_TERMS, koff, 0.0).astype(BF16)
    iwt_ref[...] = (iw_ref[...] * ((IDX_HEADS ** -0.5) * (IDX_DIM ** -0.5))).T[0:IDX_HEADS, :]

    def scores(c):
        k0 = pl.multiple_of(c * ck, ck)
        kaug_ref[:, 0:DSA_DH] = kk_ref[pl.ds(k0, ck), DKK_K0:DKK_K0 + DSA_DH]
        return _dot_nt(kaug_ref[...], qs_ref[...])

    def index_chunk(c, carry):
        k0 = pl.multiple_of(c * ck, ck)
        ik = kk_ref[pl.ds(k0, ck), DKK_IK0:DKK_IK0 + IDX_DIM]
        acc = jnp.zeros((ck, tq), F32)
        for h in range(IDX_HEADS):
            rel = jnp.maximum(_dot_nt(ik, iqs_ref[h]), 0.0)
            acc = acc + rel * iwt_ref[h:h + 1, :]
        sc = jnp.where(k0 + kloc <= qpos, acc, NEG_INF)
        isc_ref[pl.ds(k0, ck), :] = sc
        ihi_ref[pl.ds(k0, ck), :] = sc.astype(HALF)
        return carry

    lax.fori_loop(0, nch, index_chunk, 0)

    thr_ref[...] = jnp.full(thr_ref.shape, F32_LOWEST, F32)

    def count(pred, ref=isc_ref, n_chunks=nch):
        def chunk(c, a):
            k0 = pl.multiple_of(c * ck, ck)
            blk = ref[pl.ds(k0, ck), :].reshape(grp, SUBLANE, tq)
            hit = pred(blk, k0).astype(F32).reshape(grp // DSA_NACC, DSA_NACC, SUBLANE, tq)
            return a + jnp.sum(hit, axis=0)
        a = lax.fori_loop(0, n_chunks, chunk, jnp.zeros((DSA_NACC, SUBLANE, tq), F32))
        return jnp.sum(a.reshape(DSA_NACC * SUBLANE, tq), axis=0, keepdims=True)

    def key_to_float(u):
        sk = u ^ INT_MIN
        return pltpu.bitcast(jnp.where(sk >= 0, sk, sk ^ 0x7FFFFFFF), F32)

    def rows8(v):
        return jnp.broadcast_to(v, (SUBLANE, tq))[None]

    def count_high(cand_hi):
        grp16 = ck // PACKED_SUBLANE
        cb = jnp.broadcast_to(cand_hi, (PACKED_SUBLANE, tq))[None]
        one = jnp.ones((), HALF)
        zero = jnp.zeros((), HALF)

        def chunk(c, a):
            k0 = pl.multiple_of(c * ck, ck)
            blk = ihi_ref[pl.ds(k0, ck), :].reshape(grp16, PACKED_SUBLANE, tq)
            hit = jnp.where(blk >= cb, one, zero).reshape(grp16 // DSA_NACC, DSA_NACC, PACKED_SUBLANE, tq)
            for g in range(grp16 // DSA_NACC):
                a = a + hit[g]
            return a
        a = lax.fori_loop(0, nch, chunk, jnp.zeros((DSA_NACC, PACKED_SUBLANE, tq), HALF))
        return jnp.sum(a.astype(F32).reshape(DSA_NACC * PACKED_SUBLANE, tq), axis=0, keepdims=True)

    @pl.when((i + 1) * tq > topk)
    def _search():
        def half_key(u):
            return u | jnp.where((u ^ INT_MIN) < 0, 0xFFFF, 0)

        def coarse_step(it, cur):
            cand = cur | lax.shift_left(jnp.int32(1), 31 - it)
            cnt = count_high(key_to_float(half_key(cand)).astype(HALF))
            return jnp.where(cnt >= topk, cand, cur)

        coarse = lax.fori_loop(0, 16, coarse_step, jnp.zeros((1, tq), jnp.int32))
        base = half_key(coarse) - 0x8000

        span = 1 << 17

        def count_ge(u):
            cf = rows8(key_to_float(u))
            return count(lambda blk, k0: blk >= cf)

        def refine(count_fn, n_base):
            def fine_step(it, st):
                off, n_ge = st
                cand = off | lax.shift_left(jnp.int32(1), 16 - it)
                cnt = count_fn(base + cand)
                ok = cnt >= topk
                return jnp.where(ok, cand, off), jnp.where(ok, cnt, n_ge)

            off, n_ge = lax.fori_loop(0, 17, fine_step, (jnp.zeros((1, tq), jnp.int32), n_base))
            thr_ref[...] = jnp.where(full, key_to_float(base + off), F32_LOWEST)
            nge_ref[...] = n_ge

        bot8 = rows8(key_to_float(base))[0]
        top8 = rows8(key_to_float(base + span))[0]
        ngrp = ck // (DSA_CGRP * SUBLANE)
        crows = 2 * ngrp * SUBLANE

        def extract(c, st):
            a_top, a_lo = st
            k0 = pl.multiple_of(c * ck, ck)
            blk = isc_ref[pl.ds(k0, ck), :].reshape(ngrp, DSA_CGRP, SUBLANE, tq)
            best = jnp.full((ngrp, SUBLANE, tq), NEG_INF, F32)
            second = best
            for r in range(DSA_CGRP):
                x = blk[:, r]
                below = x < top8
                a_top = a_top + jnp.where(below, 0.0, 1.0)
                a_lo = a_lo + jnp.where(x >= bot8, 1.0, 0.0)
                v = jnp.where(below, x, NEG_INF)
                second = jnp.maximum(second, jnp.minimum(best, v))
                best = jnp.maximum(best, v)
            r0 = pl.multiple_of(c * crows, crows)
            cand_ref[pl.ds(r0, crows), :] = jnp.concatenate(
                [best.reshape(ngrp * SUBLANE, tq), second.reshape(ngrp * SUBLANE, tq)], axis=0)
            return a_top, a_lo

        cand_ref[...] = jnp.full(cand_ref.shape, NEG_INF, F32)
        zero_acc = jnp.zeros((ngrp, SUBLANE, tq), F32)
        a_top, a_lo = lax.fori_loop(0, nch, extract, (zero_acc, zero_acc))
        n_top = jnp.sum(a_top.reshape(ngrp * SUBLANE, tq), axis=0, keepdims=True)
        n_lo = jnp.sum(a_lo.reshape(ngrp * SUBLANE, tq), axis=0, keepdims=True)

        def count_small(u):
            cf = rows8(key_to_float(u))
            return n_top + count(lambda blk, k0: blk >= cf, cand_ref, (nch * crows + ck - 1) // ck)

        lost = jnp.max(jnp.where(full, jnp.abs(n_lo - count_small(base)), 0.0))

        @pl.when(lost == 0.0)
        def _small():
            refine(count_small, n_lo)

        @pl.when(lost > 0.0)
        def _full():
            refine(count_ge, n_lo)

        thr = thr_ref[...]
        n_ge = nge_ref[...]
        excess = jnp.max(jnp.where(full, n_ge - topk, 0.0))

        @pl.when(excess > 0.0)
        def _ties():
            thr8 = rows8(thr)
            need = topk - count(lambda blk, k0: blk > thr8)
            nbits = (seq - 1).bit_length()
            kpos3 = lax.broadcasted_iota(jnp.int32, (grp, SUBLANE, tq), 0) * SUBLANE + \
                lax.broadcasted_iota(jnp.int32, (grp, SUBLANE, tq), 1)

            def pos_step(it, p):
                cand = p | lax.shift_left(jnp.int32(1), nbits - 1 - it)
                cand8 = rows8(cand)
                cnt = count(lambda blk, k0: (blk == thr8) & (k0 + kpos3 < cand8))
                return jnp.where(cnt < need, cand, p)

            last = lax.fori_loop(0, nbits, pos_step, jnp.zeros((1, tq), jnp.int32))

            def drop(c, carry):
                k0 = pl.multiple_of(c * ck, ck)
                blk = isc_ref[pl.ds(k0, ck), :]
                kill = (blk == thr) & (k0 + kloc > last) & full
                isc_ref[pl.ds(k0, ck), :] = jnp.where(kill, NEG_INF, blk)
                return carry

            lax.fori_loop(0, nch, drop, 0)

    m_ref[...] = jnp.full(m_ref.shape, NEG_INF, F32)
    l_ref[...] = jnp.zeros_like(l_ref)
    acc_ref[...] = jnp.zeros_like(acc_ref)
    thr = thr_ref[...]
    slope_row = jnp.concatenate([jnp.full((1, tq), sl, F32) for sl in ALIBI_SLOPES], axis=1)

    def attend(c, carry):
        k0 = pl.multiple_of(c * ck, ck)
        s5 = scores(c)
        sel = isc_ref[pl.ds(k0, ck), :] >= thr
        s5 = jnp.concatenate([jnp.where(sel, s5[:, h * tq:(h + 1) * tq], NEG_INF)
                              for h in range(DSA_HEADS)], axis=1)
        cvec = slope_row * k0.astype(F32)
        m_old = m_ref[...]
        m_new = jnp.maximum(m_old, jnp.max(s5, axis=0, keepdims=True) + cvec)
        m_safe = jnp.where(m_new == NEG_INF, 0.0, m_new)
        alpha = jnp.exp(m_old - m_safe)
        p = jnp.exp(s5 - (m_safe - cvec))
        l_ref[...] = alpha * l_ref[...] + jnp.sum(p, axis=0, keepdims=True)
        acc_ref[...] = alpha * acc_ref[...] + _dot(vt_ref[:, pl.ds(k0, ck)], p.astype(BF16))
        m_ref[...] = m_new
        return carry

    lax.fori_loop(0, nch, attend, 0)

    ot_ref[DSA_W:, :] = jnp.zeros((DSA_WP - DSA_W, tq), F32)
    for h in range(DSA_HEADS):
        hs = slice(h * tq, (h + 1) * tq)
        o_lat = acc_ref[:, hs] / l_ref[:, hs]
        ot_ref[h * DSA_DH:(h + 1) * DSA_DH, :] = _dot(wuvt_ref[h], o_lat.astype(BF16))
    o_ref[...] = ot_ref[...].T[:, :DSA_W].astype(o_ref.dtype)


def _dsa(iq, dq, iw, dkk, dvt, wuvt, layer, batch, seq):
    tq = DSA_TQ
    nq = seq // tq
    topk = min(DSA_TOPK_MAX, seq // 4)
    return pl.pallas_call(
        functools.partial(_dsa_body, seq=seq, topk=topk),
        grid=(batch, nq),
        in_specs=[
            pl.BlockSpec((tq, iq.shape[1]), lambda b, i: (b * nq + i, 0)),
            pl.BlockSpec((tq, dq.shape[1]), lambda b, i: (b * nq + i, 0)),
            pl.BlockSpec((tq, iw.shape[1]), lambda b, i: (b * nq + i, 0)),
            pl.BlockSpec((seq, DKK_COLS), lambda b, i: (b, 0)),
            pl.BlockSpec((DSA_LATENT, seq), lambda b, i: (0, b)),
            _layer_block(wuvt, layer),
        ],
        out_specs=pl.BlockSpec((tq, DSA_W), lambda b, i: (b * nq + i, 0)),
        out_shape=jax.ShapeDtypeStruct((batch * seq, DSA_W), BF16),
        scratch_shapes=[
            pltpu.VMEM((seq, tq), F32),
            pltpu.VMEM((seq, tq), HALF),
            pltpu.VMEM((pl.cdiv(2 * seq // DSA_CGRP, DSA_CK) * DSA_CK, tq), F32),
            pltpu.VMEM((1, tq), F32),
            pltpu.VMEM((1, tq), F32),
            pltpu.VMEM((IDX_HEADS, tq, IDX_DIM), BF16),
            pltpu.VMEM((IDX_HEADS, tq), F32),
            pltpu.VMEM((DSA_HEADS * tq, LANE), BF16),
            pltpu.VMEM((DSA_CK, LANE), BF16),
            pltpu.VMEM((1, DSA_HEADS * tq), F32),
            pltpu.VMEM((1, DSA_HEADS * tq), F32),
            pltpu.VMEM((DSA_LATENT, DSA_HEADS * tq), F32),
            pltpu.VMEM((DSA_WP, tq), F32),
        ],
        compiler_params=_params(("parallel", "arbitrary")),
        name="dsa",
    )(iq, dq, iw, dkk, dvt, wuvt)


def _outproj_body(og_ref, od_ref, os_ref, h_ref, wg_ref, wd_ref, ws_ref, g_ref, b_ref, o_ref):
    f = _dot(og_ref[...], wg_ref[...]) + _dot(od_ref[...], wd_ref[...]) + _dot(os_ref[...], ws_ref[...])
    o_ref[...] = _ln(ALPHA * h_ref[...] + f, g_ref[...], b_ref[...])


def _memkv_body(mem_ref, w_ref, kv_ref):
    kv_ref[...] = _dot(mem_ref[...].astype(BF16), w_ref[...]).astype(kv_ref.dtype)


def _memkv(mem2d, wkv, layer, tm=512):
    m, d = mem2d.shape
    n = wkv.shape[-1]
    return pl.pallas_call(
        _memkv_body,
        grid=(m // tm,),
        in_specs=[pl.BlockSpec((tm, d), lambda i: (i, 0)), _layer_block(wkv, layer)],
        out_specs=pl.BlockSpec((tm, n), lambda i: (i, 0)),
        out_shape=jax.ShapeDtypeStruct((m, n), BF16),
        compiler_params=_params(("parallel",)),
        name="mem_kv",
    )(mem2d, wkv)


MEM_ROWS = 512


def _memattn_body(h_ref, kv_ref, wq_ref, wo_ref, g_ref, b_ref, o_ref, ctx_ref):
    tm, d = h_ref.shape
    dh = d // MEM_HEADS
    groups = [slice(r0, r0 + MEM_ROWS) for r0 in range(0, tm, MEM_ROWS)]
    qs = [_dot(h_ref[rs, :].astype(BF16), wq_ref[...]).astype(BF16) for rs in groups]

    def attend(rs, q):
        def scores(hd):
            cs = slice(hd * dh, (hd + 1) * dh)
            return _dot_nt(q[:, cs], kv_ref[:, cs]) * (dh ** -0.5)

        s = scores(0)
        ctx_prev = None
        for hd in range(MEM_HEADS):
            s_next = scores(hd + 1) if hd + 1 < MEM_HEADS else None
            s = s - jnp.max(s, axis=-1, keepdims=True)
            p = jnp.exp(s)
            p = p / jnp.sum(p, axis=-1, keepdims=True)
            ctx = _dot(p.astype(BF16), kv_ref[:, d + hd * dh:d + (hd + 1) * dh])
            if ctx_prev is not None:
                ctx_ref[rs, (hd - 1) * dh:hd * dh] = ctx_prev.astype(BF16)
            ctx_prev = ctx
            s = s_next
        ctx_ref[rs, (MEM_HEADS - 1) * dh:] = ctx_prev.astype(BF16)
        return _dot(ctx_ref[rs, :], wo_ref[...])

    fs = [attend(rs, q) for rs, q in zip(groups, qs)]
    for rs, f in zip(groups, fs):
        o_ref[rs, :] = _ln(ALPHA * h_ref[rs, :] + f, g_ref[...], b_ref[...])


MLP_FC = 512


def _mlp_body(h_ref, wu_ref, bu_ref, wd_ref, bd_ref, g_ref, b_ref, o_ref):
    h = h_ref[...]
    hb = h.astype(BF16)
    f = None
    for c0 in range(0, wu_ref.shape[1], MLP_FC):
        cs = slice(c0, c0 + MLP_FC)
        u = jnp.maximum(_dot(hb, wu_ref[:, cs]) + bu_ref[:, cs], 0.0)
        d = _dot((u * u).astype(BF16), wd_ref[cs, :])
        f = d if f is None else f + d
    o_ref[...] = _ln(ALPHA * h + (f + bd_ref[...]), g_ref[...], b_ref[...])


def _tail_body(og_ref, od_ref, os_ref, h_ref, kv_ref, wg_ref, wd_ref, ws_ref, g1_ref, b1_ref,
               wq_ref, wo_ref, g2_ref, b2_ref, wu_ref, bu_ref, wdn_ref, bdn_ref, g3_ref, b3_ref,
               o_ref, hs_ref, ctx_ref):
    _outproj_body(og_ref, od_ref, os_ref, h_ref, wg_ref, wd_ref, ws_ref, g1_ref, b1_ref, hs_ref)
    _memattn_body(hs_ref, kv_ref, wq_ref, wo_ref, g2_ref, b2_ref, hs_ref, ctx_ref)
    _mlp_body(hs_ref, wu_ref, bu_ref, wdn_ref, bdn_ref, g3_ref, b3_ref, o_ref)


def _tail(og, od, os_, h, kv, consts, layer, batch, seq, n_mem, tm=512):
    m, d = h.shape
    nt = seq // tm
    row = lambda a: pl.BlockSpec((tm, a.shape[1]), lambda bb, t: (bb * nt + t, 0))
    const = lambda a: _layer_block(a, layer, buffered=True)
    return pl.pallas_call(
        _tail_body,
        grid=(batch, nt),
        in_specs=[row(og), row(od), row(os_), row(h),
                  pl.BlockSpec((n_mem, 2 * d), lambda bb, t: (bb, 0))] + [const(a) for a in consts],
        out_specs=pl.BlockSpec((tm, d), lambda bb, t: (bb * nt + t, 0)),
        out_shape=jax.ShapeDtypeStruct((m, d), F32),
        scratch_shapes=[pltpu.VMEM((tm, d), F32), pltpu.VMEM((tm, d), BF16)],
        compiler_params=_params(("parallel", "parallel")),
        name="tail",
    )(og, od, os_, h, kv, *consts)


def _pad_heads(w, heads, width, padded):
    lead = w.shape[:-1]
    w = w.reshape(lead + (heads, width))
    w = jnp.pad(w, [(0, 0)] * len(lead) + [(0, 0), (0, padded - width)])
    return w.reshape(lead + (heads * padded,))


def _pad_last(w, n):
    return jnp.pad(w, [(0, 0)] * (w.ndim - 1) + [(0, n - w.shape[-1])])


def _split_w_in(w_in):
    sizes = (GLA_HEADS * GLA_DK, GLA_HEADS * GLA_DK, GLA_HEADS * GLA_DV, GLA_RANK, GLA_HEADS * GLA_DV,
             DSA_HEADS * DSA_DH, DSA_DH, DSA_LATENT, IDX_HEADS * IDX_DIM, IDX_DIM, IDX_HEADS,
             SB_HEADS * SB_DH, SB_HEADS * SB_DH, SB_HEADS * SB_DH)
    w_in = w_in.astype(BF16)
    parts = []
    o = 0
    for s in sizes:
        parts.append(w_in[..., o:o + s])
        o += s
    gq, gk, gv, glr, gg, dq, dk, dv, iq, ik, iw, sq, sk, sv = parts
    pieces = {
        "gla": jnp.concatenate([
            _pad_heads(gq, GLA_HEADS, GLA_DK, GLA_DKP), _pad_heads(gk, GLA_HEADS, GLA_DK, GLA_DKP),
            _pad_heads(gg, GLA_HEADS, GLA_DV, GLA_DVP)], axis=-1),
        "gv": _pad_heads(gv, GLA_HEADS, GLA_DV, GLA_DVP),
        "glr": _pad_last(glr, LANE),
        "iq": iq,
        "dkk": jnp.concatenate([ik, dk], axis=-1),
        "iw": _pad_last(iw, LANE),
        "dv": dv,
        "dq": _pad_last(dq, PAD_W),
        "sq": _pad_last(sq, PAD_W),
        "sk": _pad_last(sk, PAD_W),
        "sv": _pad_last(sv, PAD_W),
    }
    for name, width, _, _, _ in INPROJ_STREAMS:
        assert pieces[name].shape[-1] == width, name
    return jnp.concatenate([pieces[s[0]] for s in INPROJ_STREAMS], axis=-1)


def kernel(x, mem, ln_in_g, ln_in_b, w_in, gla_gate_w2, gla_gate_b, gla_norm_g, dsa_w_uv, w_out,
           ln_mix_g, ln_mix_b, w_mem_q, w_mem_kv, w_mem_o, ln_mem_g, ln_mem_b,
           w_up, b_up, w_down, b_down, ln_ffn_g, ln_ffn_b):
    batch, seq, d = x.shape
    n_mem = mem.shape[1]
    depth = w_in.shape[0]
    x2 = x.reshape(batch * seq, d)
    mem2 = mem.reshape(batch * n_mem, d)
    rows = lambda a: a.reshape(depth, 1, -1)
    gw, dw = GLA_HEADS * GLA_DV, DSA_HEADS * DSA_DH

    w_all = _split_w_in(w_in)
    w2p = jnp.pad(_pad_heads(gla_gate_w2, GLA_HEADS, GLA_DK, GLA_DKP),
                  ((0, 0), (0, LANE - GLA_RANK), (0, 0))).astype(BF16)
    gbp = rows(_pad_heads(gla_gate_b, GLA_HEADS, GLA_DK, GLA_DKP))
    ngp = rows(_pad_last(gla_norm_g, GLA_DVP))
    wuvt = jnp.swapaxes(dsa_w_uv, 2, 3).astype(BF16)
    wo = w_out.astype(BF16)
    wg = jnp.pad(wo[:, :gw].reshape(depth, GLA_HEADS, GLA_DV, d), ((0, 0), (0, 0), (0, GLA_DVP - GLA_DV), (0, 0)))
    wg = wg.reshape(depth, GLA_HEADS * GLA_DVP, d)
    wkv = w_mem_kv.astype(BF16)
    consts = (wg, wo[:, gw:gw + dw], wo[:, gw + dw:], rows(ln_mix_g), rows(ln_mix_b),
              w_mem_q.astype(BF16), w_mem_o.astype(BF16), rows(ln_mem_g), rows(ln_mem_b),
              w_up.astype(BF16), rows(b_up), w_down.astype(BF16), rows(b_down),
              rows(ln_ffn_g), rows(ln_ffn_b))

    h = None
    for l in range(depth):
        if l == 0:
            outs = _inproj(x2, w_all, l, ln=(ln_in_g, ln_in_b))
            h = outs[-1]
            outs = outs[:-1]
        else:
            outs = _inproj(h, w_all, l)
        p_gla, p_gv, p_gvt, p_glr, p_iq, p_dkk, p_iw, p_dvt, p_dq, p_sq, p_sk, p_svt = outs
        o_gla = _gla(p_gla, p_gv, p_gvt, p_glr, w2p, gbp, ngp, l, batch, seq)
        o_dsa = _dsa(p_iq, p_dq, p_iw, p_dkk, p_dvt, wuvt, l, batch, seq)
        o_sb = _sb(p_sq, p_sk, p_svt, batch, seq)
        kv = _memkv(mem2, wkv, l)
        h = _tail(o_gla, o_dsa, o_sb, h, kv, consts, l, batch, seq, n_mem)
    return h.reshape(batch, seq, d)
```

```python
import functools

import jax
import jax.numpy as jnp
import numpy as np
from jax import lax
from jax.experimental import pallas as pl
from jax.experimental.pallas import tpu as pltpu

F32 = jnp.float32
BF16 = jnp.bfloat16
HALF = jnp.bfloat16

DEPTH = 2
LN_EPS = 1e-5
GLA_HEADS, GLA_DK, GLA_DV, GLA_RANK, GLA_TAU, GLA_CHUNK = 4, 48, 96, 16, 16.0, 64
GLA_DKP, GLA_DVP = 64, 128
DSA_HEADS, DSA_DH, DSA_LATENT = 5, 64, 128
IDX_HEADS, IDX_DIM, DSA_TOPK_MAX = 8, 64, 256
SB_HEADS, SB_DH = 5, 64
MEM_HEADS = 4
ALPHA = (2.0 * DEPTH) ** 0.25
ALIBI_SLOPES = tuple(2.0 ** (-8.0 * (i + 1) / DSA_HEADS) for i in range(DSA_HEADS))
ALIBI_TERMS = 3
NEG_INF = float("-inf")
F32_LOWEST = float(jnp.finfo(jnp.float32).min)
INT_MIN = -(2 ** 31)

LANE = 128
SUBLANE = 8
PACKED_SUBLANE = 16
VMEM_LIMIT = 56 * 1024 * 1024


def _dot(a, b):
    return jnp.dot(a, b, preferred_element_type=F32)


def _dot_nt(a, b):
    return lax.dot_general(a, b, (((1,), (1,)), ((), ())), preferred_element_type=F32)


def _ln(x, g, b):
    mu = jnp.mean(x, axis=-1, keepdims=True)
    xc = x - mu
    var = jnp.mean(xc * xc, axis=-1, keepdims=True)
    return xc * lax.rsqrt(var + LN_EPS) * g + b


def _softplus(x):
    return jnp.maximum(x, 0.0) + jnp.log(1.0 + jnp.exp(-jnp.abs(x)))


def _split_dot_left(m, x, terms):
    out = None
    r = x
    for t in range(terms):
        xb = r.astype(BF16)
        d = _dot(m, xb)
        out = d if out is None else out + d
        if t + 1 < terms:
            r = r - xb.astype(F32)
    return out


def _bf16_terms(x, n):
    terms = []
    r = np.float32(x)
    for _ in range(n):
        t = np.float32(np.asarray(r).astype(jnp.bfloat16))
        terms.append(float(t))
        r = np.float32(r - t)
    return tuple(terms)


def _layer_block(a, layer, buffered=False):
    zeros = (0,) * (a.ndim - 1)
    kw = dict(pipeline_mode=pl.Buffered(1)) if buffered else {}
    return pl.BlockSpec((None,) + a.shape[1:], lambda *_: (layer,) + zeros, **kw)


def _params(sem):
    return pltpu.CompilerParams(dimension_semantics=sem, vmem_limit_bytes=VMEM_LIMIT)


PAD_W = 384
INPROJ_CHUNK = 512
INPROJ_STREAMS = (
    ("gla", 2 * GLA_HEADS * GLA_DKP + GLA_HEADS * GLA_DVP, F32, True, False),
    ("gv", GLA_HEADS * GLA_DVP, BF16, True, True),
    ("glr", LANE, BF16, True, False),
    ("iq", IDX_HEADS * IDX_DIM, BF16, True, False),
    ("dkk", IDX_DIM + DSA_DH, BF16, True, False),
    ("iw", LANE, F32, True, False),
    ("dv", DSA_LATENT, BF16, False, True),
    ("dq", PAD_W, BF16, True, False),
    ("sq", PAD_W, BF16, True, False),
    ("sk", PAD_W, BF16, True, False),
    ("sv", PAD_W, BF16, False, True),
)
INPROJ_COLS = sum(s[1] for s in INPROJ_STREAMS)


def _inproj_outputs():
    outs = []
    for si, (_, _, _, plain, tr) in enumerate(INPROJ_STREAMS):
        if plain:
            outs.append((si, False))
        if tr:
            outs.append((si, True))
    return outs


def _inproj_body(*refs, apply_ln):
    if apply_ln:
        x_ref, g_ref, b_ref, w_ref = refs[:4]
        o_refs = refs[4:]
    else:
        x_ref, w_ref = refs[:2]
        o_refs = refs[2:]
    x = x_ref[...]
    if apply_ln:
        x = _ln(x, g_ref[...], b_ref[...])
        o_refs[-1][...] = x
    xb = x.astype(BF16)
    starts = np.cumsum([0] + [s[1] for s in INPROJ_STREAMS])
    outs = _inproj_outputs()
    for c0 in range(0, INPROJ_COLS, INPROJ_CHUNK):
        c1 = c0 + INPROJ_CHUNK
        y = _dot(xb, w_ref[:, c0:c1])
        for o_ref, (si, tr) in zip(o_refs, outs):
            a, b = max(int(starts[si]), c0), min(int(starts[si + 1]), c1)
            if a >= b:
                continue
            part = y[:, a - c0:b - c0]
            lo, hi = a - int(starts[si]), b - int(starts[si])
            if tr:
                o_ref[lo:hi, :] = part.T.astype(o_ref.dtype)
            else:
                o_ref[:, lo:hi] = part.astype(o_ref.dtype)


def _inproj(x, w_all, layer, ln=None, tm=512):
    m, d = x.shape
    apply_ln = ln is not None
    in_specs = [pl.BlockSpec((tm, d), lambda i: (i, 0))]
    args = [x]
    if apply_ln:
        in_specs += [pl.BlockSpec((1, d), lambda i: (0, 0))] * 2
        args += [ln[0].reshape(1, d), ln[1].reshape(1, d)]
    in_specs.append(_layer_block(w_all, layer))
    args.append(w_all)
    out_shape, out_specs = [], []
    for si, tr in _inproj_outputs():
        _, width, dt, _, _ = INPROJ_STREAMS[si]
        if tr:
            out_shape.append(jax.ShapeDtypeStruct((width, m), dt))
            out_specs.append(pl.BlockSpec((width, tm), lambda i: (0, i)))
        else:
            out_shape.append(jax.ShapeDtypeStruct((m, width), dt))
            out_specs.append(pl.BlockSpec((tm, width), lambda i: (i, 0)))
    if apply_ln:
        out_shape.append(jax.ShapeDtypeStruct((m, d), F32))
        out_specs.append(pl.BlockSpec((tm, d), lambda i: (i, 0)))
    return pl.pallas_call(
        functools.partial(_inproj_body, apply_ln=apply_ln),
        grid=(m // tm,),
        in_specs=in_specs,
        out_specs=out_specs,
        out_shape=out_shape,
        compiler_params=_params(("parallel",)),
        name="inproj_ln" if apply_ln else "inproj",
    )(*args)


GLA_KW = GLA_HEADS * GLA_DKP
GLA_VW = GLA_HEADS * GLA_DVP
GLA_Q0, GLA_K0, GLA_G0 = 0, GLA_KW, 2 * GLA_KW
GLA_COLS = GLA_G0 + GLA_VW
GLA_TL = 256
GLA_NC = GLA_TL // GLA_CHUNK


def _gla_body(x_ref, v_ref, vt_ref, r_ref, w2_ref, gb_ref, ng_ref, o_ref, s_ref):
    @pl.when(pl.program_id(1) == 0)
    def _():
        s_ref[...] = jnp.zeros_like(s_ref)

    tl, c, nc = GLA_TL, GLA_CHUNK, GLA_NC
    row = lax.broadcasted_iota(jnp.int32, (tl, tl), 0)
    col = lax.broadcasted_iota(jnp.int32, (tl, tl), 1)
    rch = jnp.floor(row.astype(F32) * (1.0 / c))
    cch = jnp.floor(col.astype(F32) * (1.0 / c))
    causal = (rch == cch) & (col <= row)
    tri = causal.astype(BF16)
    heads = range(GLA_HEADS)
    ksl = [slice(h * GLA_DKP, (h + 1) * GLA_DKP) for h in heads]
    vsl = [slice(h * GLA_DVP, (h + 1) * GLA_DVP) for h in heads]

    xg = _dot(r_ref[...], w2_ref[...]) + gb_ref[...]
    log_a = -_softplus(-xg) / GLA_TAU
    bcum = _split_dot_left(tri, log_a, 3)
    krow = lax.broadcasted_iota(jnp.int32, (tl, GLA_KW), 0)
    blast = bcum[tl - 1:tl, :]
    for j in range(nc - 2, -1, -1):
        blast = jnp.where(krow < (j + 1) * c, bcum[(j + 1) * c - 1:(j + 1) * c, :], blast)
    q = x_ref[:, GLA_Q0:GLA_Q0 + GLA_KW]
    k = x_ref[:, GLA_K0:GLA_K0 + GLA_KW]
    q_t = (q * (GLA_DK ** -0.5) * jnp.exp(bcum)).astype(BF16)
    k_t = (k * jnp.exp(-bcum)).astype(BF16)
    k_end = (k * jnp.exp(blast - bcum)).astype(BF16)
    dec = [jnp.exp(bcum[(j + 1) * c - 1:(j + 1) * c, :]) for j in range(nc)]

    lane_chunk = jnp.floor(lax.broadcasted_iota(jnp.int32, (GLA_DVP, tl), 1).astype(F32) * (1.0 / c))
    o_intra, ds = [], []
    for h in heads:
        sc = jnp.where(causal, _dot_nt(q_t[:, ksl[h]], k_t[:, ksl[h]]), 0.0)
        o_intra.append(_dot(sc.astype(BF16), v_ref[:, vsl[h]]))
        vt = vt_ref[vsl[h], :]
        zero = jnp.zeros_like(vt)
        vt_by_chunk = jnp.concatenate([jnp.where(lane_chunk == float(j), vt, zero) for j in range(nc)], axis=0)
        ds.append(_dot(vt_by_chunk, k_end[:, ksl[h]]))

    ng = ng_ref[...]
    for h in heads:
        s = s_ref[h]
        states = []
        for j in range(nc):
            states.append(s.astype(BF16))
            s = s * dec[j][:, ksl[h]] + ds[h][j * GLA_DVP:(j + 1) * GLA_DVP, :]
        s_ref[h] = s
        inter_all = _dot_nt(q_t[:, ksl[h]], jnp.concatenate(states, axis=0))
        o_inter = jnp.concatenate(
            [inter_all[j * c:(j + 1) * c, j * GLA_DVP:(j + 1) * GLA_DVP] for j in range(nc)], axis=0)
        o = o_intra[h] + o_inter
        ms = jnp.sum(o * o, axis=-1, keepdims=True) * (1.0 / GLA_DV)
        on = o * lax.rsqrt(ms + LN_EPS) * ng
        g = x_ref[:, GLA_G0 + h * GLA_DVP:GLA_G0 + (h + 1) * GLA_DVP]
        o_ref[:, vsl[h]] = (on * (g / (1.0 + jnp.exp(-g)))).astype(o_ref.dtype)


def _gla(x32, v, vt, glr, w2p, gbp, ngp, layer, batch, seq):
    tl = GLA_TL
    nt = seq // tl
    const = lambda a: _layer_block(a, layer)
    return pl.pallas_call(
        _gla_body,
        grid=(batch, nt),
        in_specs=[
            pl.BlockSpec((tl, GLA_COLS), lambda b, t: (b * nt + t, 0)),
            pl.BlockSpec((tl, GLA_VW), lambda b, t: (b * nt + t, 0)),
            pl.BlockSpec((GLA_VW, tl), lambda b, t: (0, b * nt + t)),
            pl.BlockSpec((tl, LANE), lambda b, t: (b * nt + t, 0)),
            const(w2p), const(gbp), const(ngp),
        ],
        out_specs=pl.BlockSpec((tl, GLA_VW), lambda b, t: (b * nt + t, 0)),
        out_shape=jax.ShapeDtypeStruct((batch * seq, GLA_VW), BF16),
        scratch_shapes=[pltpu.VMEM((GLA_HEADS, GLA_DVP, GLA_DKP), F32)],
        compiler_params=_params(("arbitrary", "arbitrary")),
        name="gla",
    )(x32, v, vt, glr, w2p, gbp, ngp)


SB_TQ = 256
SB_CK = 256
SB_W = SB_HEADS * SB_DH
SB_WP = 384


def _sb_body(q_ref, k_ref, vt_ref, o_ref, qs_ref, acc_ref, run_ref):
    tq, ck = SB_TQ, SB_CK
    i = pl.program_id(1)
    c_last = (i + 1) * (tq // ck) - 1
    kloc = lax.broadcasted_iota(jnp.int32, (ck, tq), 0)
    qloc = lax.broadcasted_iota(jnp.int32, (ck, tq), 1)
    qpos = i * tq + qloc
    later = (lax.broadcasted_iota(jnp.int32, (ck, ck), 1) > lax.broadcasted_iota(jnp.int32, (ck, ck), 0)).astype(BF16)
    for h in range(SB_HEADS):
        qs_ref[h] = q_ref[:, h * SB_DH:(h + 1) * SB_DH] * (SB_DH ** -0.5)
    acc_ref[...] = jnp.zeros_like(acc_ref)
    run_ref[...] = jnp.zeros_like(run_ref)

    heads = range(SB_HEADS)
    hsl = [slice(h * SB_DH, (h + 1) * SB_DH) for h in heads]

    def chunk(c, masked):
        k0 = pl.multiple_of(c * ck, ck)
        zs = [_dot_nt(k_ref[pl.ds(k0, ck), hsl[h]], qs_ref[h]) for h in heads]
        runs = run_ref[...]
        sps = [_softplus(z) for z in zs]
        l1s = [-sp for sp in sps]
        if masked:
            valid = k0 + kloc < qpos
            l1s = [jnp.where(valid, l1, 0.0) for l1 in l1s]
        suf = _split_dot_left(later, jnp.concatenate(l1s, axis=1), 2)
        new_runs = []
        for h in heads:
            w = jnp.exp(zs[h] - sps[h] + suf[:, h * tq:(h + 1) * tq] + runs[h:h + 1, :])
            if masked:
                w = jnp.where(valid, w, 0.0)
            acc_ref[hsl[h], :] += _dot(vt_ref[hsl[h], pl.ds(k0, ck)], w.astype(BF16))
            new_runs.append(runs[h:h + 1, :] + jnp.sum(l1s[h], axis=0, keepdims=True))
        new_runs = jnp.concatenate(new_runs, axis=0)
        run_ref[0:SB_HEADS, :] = new_runs
        return (jnp.max(jnp.exp(new_runs + 2.0)) > 0.0).astype(jnp.int32)

    n_diag = tq // ck
    alive = jnp.int32(1)
    for j in range(n_diag):
        alive = chunk(c_last - j, True)

    def cond(st):
        step, alive = st
        return (step <= c_last) & (alive > 0)

    def body(st):
        step, _ = st
        return step + 1, chunk(c_last - step, False)

    lax.while_loop(cond, body, (jnp.int32(n_diag), alive))
    o_ref[...] = acc_ref[...].T[:, :SB_W].astype(o_ref.dtype)


def _sb(q, k, vt, batch, seq):
    tq = SB_TQ
    nq = seq // tq
    return pl.pallas_call(
        _sb_body,
        grid=(batch, nq),
        in_specs=[
            pl.BlockSpec((tq, q.shape[1]), lambda b, i: (b * nq + i, 0)),
            pl.BlockSpec((seq, k.shape[1]), lambda b, i: (b, 0)),
            pl.BlockSpec((vt.shape[0], seq), lambda b, i: (0, b)),
        ],
        out_specs=pl.BlockSpec((tq, SB_W), lambda b, i: (b * nq + i, 0)),
        out_shape=jax.ShapeDtypeStruct((batch * seq, SB_W), BF16),
        scratch_shapes=[
            pltpu.VMEM((SB_HEADS, tq, SB_DH), BF16),
            pltpu.VMEM((SB_WP, tq), F32),
            pltpu.VMEM((SUBLANE, tq), F32),
        ],
        compiler_params=_params(("parallel", "arbitrary")),
        name="sb",
    )(q, k, vt)


DSA_TQ = 256
DSA_CK = 256
DSA_NACC = 4
DSA_CGRP = 8
DSA_W = DSA_HEADS * DSA_DH
DSA_WP = 384
DKK_IK0, DKK_K0 = 0, IDX_DIM
DKK_COLS = IDX_DIM + DSA_DH


def _dsa_body(iq_ref, dq_ref, iw_ref, kk_ref, vt_ref, wuvt_ref, o_ref,
              isc_ref, ihi_ref, cand_ref, thr_ref, nge_ref, iqs_ref, iwt_ref, qs_ref, kaug_ref, sraw_ref,
              m_ref, l_ref, acc_ref, ot_ref,
              *, seq, topk):
    tq, ck = DSA_TQ, DSA_CK
    grp = ck // SUBLANE
    i = pl.program_id(1)
    nch = (i + 1) * (tq // ck)
    kloc = lax.broadcasted_iota(jnp.int32, (ck, tq), 0)
    qloc = lax.broadcasted_iota(jnp.int32, (ck, tq), 1)
    qpos = i * tq + qloc
    rowpos = i * tq + lax.broadcasted_iota(jnp.int32, (1, tq), 1)
    full = rowpos >= topk

    for h in range(IDX_HEADS):
        iqs_ref[h] = iq_ref[:, h * IDX_DIM:(h + 1) * IDX_DIM]
    qlane = lax.broadcasted_iota(jnp.int32, (tq, LANE - DSA_DH), 1)
    for h in range(DSA_HEADS):
        hr = slice(h * tq, (h + 1) * tq)
        qs_ref[hr, 0:DSA_DH] = dq_ref[:, h * DSA_DH:(h + 1) * DSA_DH] * (DSA_DH ** -0.5)
        pieces = jnp.zeros((tq, LANE - DSA_DH), F32)
        for t, piece in enumerate(_bf16_terms(ALIBI_SLOPES[h], ALIBI_TERMS)):
            pieces = jnp.where(qlane == t, piece, pieces)
        qs_ref[hr, DSA_DH:LANE] = pieces.astype(BF16)
    klane = lax.broadcasted_iota(jnp.int32, (ck, LANE - DSA_DH), 1)
    koff = lax.broadcasted_iota(jnp.int32, (ck, LANE - DSA_DH), 0).astype(F32)
    kaug_ref[:, DSA_DH:LANE] = jnp.where(klane < ALIBI_TERMS, koff, 0.0).astype(BF16)
    iwt_ref[...] = (iw_ref[...] * ((IDX_HEADS ** -0.5) * (IDX_DIM ** -0.5))).T[0:IDX_HEADS, :]

    def scores(c):
        k0 = pl.multiple_of(c * ck, ck)
        kaug_ref[:, 0:DSA_DH] = kk_ref[pl.ds(k0, ck), DKK_K0:DKK_K0 + DSA_DH]
        return _dot_nt(kaug_ref[...], qs_ref[...])

    sraw_ref[...] = scores(0)

    def index_chunk(c, carry):
        k0 = pl.multiple_of(c * ck, ck)
        ik = kk_ref[pl.ds(k0, ck), DKK_IK0:DKK_IK0 + IDX_DIM]
        acc = jnp.zeros((ck, tq), F32)
        for h in range(IDX_HEADS):
            rel = jnp.maximum(_dot_nt(ik, iqs_ref[h]), 0.0)
            acc = acc + rel * iwt_ref[h:h + 1, :]
        sc = jnp.where(k0 + kloc <= qpos, acc, NEG_INF)
        isc_ref[pl.ds(k0, ck), :] = sc
        ihi_ref[pl.ds(k0, ck), :] = sc.astype(HALF)
        return carry

    lax.fori_loop(0, nch, index_chunk, 0)

    thr_ref[...] = jnp.full(thr_ref.shape, F32_LOWEST, F32)

    def count(pred, ref=isc_ref, n_chunks=nch):
        def chunk(c, a):
            k0 = pl.multiple_of(c * ck, ck)
            blk = ref[pl.ds(k0, ck), :].reshape(grp, SUBLANE, tq)
            hit = pred(blk, k0).astype(F32).reshape(grp // DSA_NACC, DSA_NACC, SUBLANE, tq)
            return a + jnp.sum(hit, axis=0)
        a = lax.fori_loop(0, n_chunks, chunk, jnp.zeros((DSA_NACC, SUBLANE, tq), F32))
        return jnp.sum(a.reshape(DSA_NACC * SUBLANE, tq), axis=0, keepdims=True)

    def key_to_float(u):
        sk = u ^ INT_MIN
        return pltpu.bitcast(jnp.where(sk >= 0, sk, sk ^ 0x7FFFFFFF), F32)

    def rows8(v):
        return jnp.broadcast_to(v, (SUBLANE, tq))[None]

    def count_high(cand_hi):
        grp16 = ck // PACKED_SUBLANE
        cb = jnp.broadcast_to(cand_hi, (PACKED_SUBLANE, tq))[None]
        one = jnp.ones((), HALF)
        zero = jnp.zeros((), HALF)

        def chunk(c, a):
            k0 = pl.multiple_of(c * ck, ck)
            blk = ihi_ref[pl.ds(k0, ck), :].reshape(grp16, PACKED_SUBLANE, tq)
            hit = jnp.where(blk >= cb, one, zero).reshape(grp16 // DSA_NACC, DSA_NACC, PACKED_SUBLANE, tq)
            for g in range(grp16 // DSA_NACC):
                a = a + hit[g]
            return a
        a = lax.fori_loop(0, nch, chunk, jnp.zeros((DSA_NACC, PACKED_SUBLANE, tq), HALF))
        return jnp.sum(a.astype(F32).reshape(DSA_NACC * PACKED_SUBLANE, tq), axis=0, keepdims=True)

    @pl.when((i + 1) * tq > topk)
    def _search():
        def half_key(u):
            return u | jnp.where((u ^ INT_MIN) < 0, 0xFFFF, 0)

        def coarse_step(it, cur):
            cand = cur | lax.shift_left(jnp.int32(1), 31 - it)
            cnt = count_high(key_to_float(half_key(cand)).astype(HALF))
            return jnp.where(cnt >= topk, cand, cur)

        coarse = lax.fori_loop(0, 16, coarse_step, jnp.zeros((1, tq), jnp.int32))
        base = half_key(coarse) - 0x8000

        span = 1 << 17

        def count_ge(u):
            cf = rows8(key_to_float(u))
            return count(lambda blk, k0: blk >= cf)

        def refine(count_fn, n_base):
            def fine_step(it, st):
                off, n_ge = st
                cand = off | lax.shift_left(jnp.int32(1), 16 - it)
                cnt = count_fn(base + cand)
                ok = cnt >= topk
                return jnp.where(ok, cand, off), jnp.where(ok, cnt, n_ge)

            off, n_ge = lax.fori_loop(0, 17, fine_step, (jnp.zeros((1, tq), jnp.int32), n_base))
            thr_ref[...] = jnp.where(full, key_to_float(base + off), F32_LOWEST)
            nge_ref[...] = n_ge

        bot8 = rows8(key_to_float(base))[0]
        top8 = rows8(key_to_float(base + span))[0]
        ngrp = ck // (DSA_CGRP * SUBLANE)
        crows = 2 * ngrp * SUBLANE

        def extract(c, st):
            a_top, a_lo = st
            k0 = pl.multiple_of(c * ck, ck)
            blk = isc_ref[pl.ds(k0, ck), :].reshape(ngrp, DSA_CGRP, SUBLANE, tq)
            best = jnp.full((ngrp, SUBLANE, tq), NEG_INF, F32)
            second = best
            for r in range(DSA_CGRP):
                x = blk[:, r]
                below = x < top8
                a_top = a_top + jnp.where(below, 0.0, 1.0)
                a_lo = a_lo + jnp.where(x >= bot8, 1.0, 0.0)
                v = jnp.where(below, x, NEG_INF)
                second = jnp.maximum(second, jnp.minimum(best, v))
                best = jnp.maximum(best, v)
            r0 = pl.multiple_of(c * crows, crows)
            cand_ref[pl.ds(r0, crows), :] = jnp.concatenate(
                [best.reshape(ngrp * SUBLANE, tq), second.reshape(ngrp * SUBLANE, tq)], axis=0)
            return a_top, a_lo

        cand_ref[...] = jnp.full(cand_ref.shape, NEG_INF, F32)
        zero_acc = jnp.zeros((ngrp, SUBLANE, tq), F32)
        a_top, a_lo = lax.fori_loop(0, nch, extract, (zero_acc, zero_acc))
        n_top = jnp.sum(a_top.reshape(ngrp * SUBLANE, tq), axis=0, keepdims=True)
        n_lo = jnp.sum(a_lo.reshape(ngrp * SUBLANE, tq), axis=0, keepdims=True)

        def count_small(u):
            cf = rows8(key_to_float(u))
            return n_top + count(lambda blk, k0: blk >= cf, cand_ref, (nch * crows + ck - 1) // ck)

        lost = jnp.max(jnp.where(full, jnp.abs(n_lo - count_small(base)), 0.0))

        @pl.when(lost == 0.0)
        def _small():
            refine(count_small, n_lo)

        @pl.when(lost > 0.0)
        def _full():
            refine(count_ge, n_lo)

        thr = thr_ref[...]
        n_ge = nge_ref[...]
        excess = jnp.max(jnp.where(full, n_ge - topk, 0.0))

        @pl.when(excess > 0.0)
        def _ties():
            thr8 = rows8(thr)
            need = topk - count(lambda blk, k0: blk > thr8)
            nbits = (seq - 1).bit_length()
            kpos3 = lax.broadcasted_iota(jnp.int32, (grp, SUBLANE, tq), 0) * SUBLANE + \
                lax.broadcasted_iota(jnp.int32, (grp, SUBLANE, tq), 1)

            def pos_step(it, p):
                cand = p | lax.shift_left(jnp.int32(1), nbits - 1 - it)
                cand8 = rows8(cand)
                cnt = count(lambda blk, k0: (blk == thr8) & (k0 + kpos3 < cand8))
                return jnp.where(cnt < need, cand, p)

            last = lax.fori_loop(0, nbits, pos_step, jnp.zeros((1, tq), jnp.int32))

            def drop(c, carry):
                k0 = pl.multiple_of(c * ck, ck)
                blk = isc_ref[pl.ds(k0, ck), :]
                kill = (blk == thr) & (k0 + kloc > last) & full
                isc_ref[pl.ds(k0, ck), :] = jnp.where(kill, NEG_INF, blk)
                return carry

            lax.fori_loop(0, nch, drop, 0)

    m_ref[...] = jnp.full(m_ref.shape, NEG_INF, F32)
    l_ref[...] = jnp.zeros_like(l_ref)
    acc_ref[...] = jnp.zeros_like(acc_ref)
    thr = thr_ref[...]
    slope_row = jnp.concatenate([jnp.full((1, tq), sl, F32) for sl in ALIBI_SLOPES], axis=1)

    def attend(c, carry):
        k0 = pl.multiple_of(c * ck, ck)
        s5 = sraw_ref[...]
        s_next = scores(jnp.minimum(c + 1, nch - 1))
        sel = isc_ref[pl.ds(k0, ck), :] >= thr
        s5 = jnp.concatenate([jnp.where(sel, s5[:, h * tq:(h + 1) * tq], NEG_INF)
                              for h in range(DSA_HEADS)], axis=1)
        cvec = slope_row * k0.astype(F32)
        m_old = m_ref[...]
        m_new = jnp.maximum(m_old, jnp.max(s5, axis=0, keepdims=True) + cvec)
        m_safe = jnp.where(m_new == NEG_INF, 0.0, m_new)
        alpha = jnp.exp(m_old - m_safe)
        p = jnp.exp(s5 - (m_safe - cvec))
        l_ref[...] = alpha * l_ref[...] + jnp.sum(p, axis=0, keepdims=True)
        acc_ref[...] = alpha * acc_ref[...] + _dot(vt_ref[:, pl.ds(k0, ck)], p.astype(BF16))
        m_ref[...] = m_new
        sraw_ref[...] = s_next
        return carry

    lax.fori_loop(0, nch, attend, 0)

    ot_ref[DSA_W:, :] = jnp.zeros((DSA_WP - DSA_W, tq), F32)
    for h in range(DSA_HEADS):
        hs = slice(h * tq, (h + 1) * tq)
        o_lat = acc_ref[:, hs] / l_ref[:, hs]
        ot_ref[h * DSA_DH:(h + 1) * DSA_DH, :] = _dot(wuvt_ref[h], o_lat.astype(BF16))
    o_ref[...] = ot_ref[...].T[:, :DSA_W].astype(o_ref.dtype)


def _dsa(iq, dq, iw, dkk, dvt, wuvt, layer, batch, seq):
    tq = DSA_TQ
    nq = seq // tq
    topk = min(DSA_TOPK_MAX, seq // 4)
    return pl.pallas_call(
        functools.partial(_dsa_body, seq=seq, topk=topk),
        grid=(batch, nq),
        in_specs=[
            pl.BlockSpec((tq, iq.shape[1]), lambda b, i: (b * nq + i, 0)),
            pl.BlockSpec((tq, dq.shape[1]), lambda b, i: (b * nq + i, 0)),
            pl.BlockSpec((tq, iw.shape[1]), lambda b, i: (b * nq + i, 0)),
            pl.BlockSpec((seq, DKK_COLS), lambda b, i: (b, 0)),
            pl.BlockSpec((DSA_LATENT, seq), lambda b, i: (0, b)),
            _layer_block(wuvt, layer),
        ],
        out_specs=pl.BlockSpec((tq, DSA_W), lambda b, i: (b * nq + i, 0)),
        out_shape=jax.ShapeDtypeStruct((batch * seq, DSA_W), BF16),
        scratch_shapes=[
            pltpu.VMEM((seq, tq), F32),
            pltpu.VMEM((seq, tq), HALF),
            pltpu.VMEM((pl.cdiv(2 * seq // DSA_CGRP, DSA_CK) * DSA_CK, tq), F32),
            pltpu.VMEM((1, tq), F32),
            pltpu.VMEM((1, tq), F32),
            pltpu.VMEM((IDX_HEADS, tq, IDX_DIM), BF16),
            pltpu.VMEM((IDX_HEADS, tq), F32),
            pltpu.VMEM((DSA_HEADS * tq, LANE), BF16),
            pltpu.VMEM((DSA_CK, LANE), BF16),
            pltpu.VMEM((DSA_CK, DSA_HEADS * tq), F32),
            pltpu.VMEM((1, DSA_HEADS * tq), F32),
            pltpu.VMEM((1, DSA_HEADS * tq), F32),
            pltpu.VMEM((DSA_LATENT, DSA_HEADS * tq), F32),
            pltpu.VMEM((DSA_WP, tq), F32),
        ],
        compiler_params=_params(("parallel", "arbitrary")),
        name="dsa",
    )(iq, dq, iw, dkk, dvt, wuvt)


def _outproj_body(og_ref, od_ref, os_ref, h_ref, wg_ref, wd_ref, ws_ref, g_ref, b_ref, o_ref):
    f = _dot(og_ref[...], wg_ref[...]) + _dot(od_ref[...], wd_ref[...]) + _dot(os_ref[...], ws_ref[...])
    o_ref[...] = _ln(ALPHA * h_ref[...] + f, g_ref[...], b_ref[...])


def _memkv_body(mem_ref, w_ref, kv_ref):
    kv_ref[...] = _dot(mem_ref[...].astype(BF16), w_ref[...]).astype(kv_ref.dtype)


def _memkv(mem2d, wkv, layer, tm=512):
    m, d = mem2d.shape
    n = wkv.shape[-1]
    return pl.pallas_call(
        _memkv_body,
        grid=(m // tm,),
        in_specs=[pl.BlockSpec((tm, d), lambda i: (i, 0)), _layer_block(wkv, layer)],
        out_specs=pl.BlockSpec((tm, n), lambda i: (i, 0)),
        out_shape=jax.ShapeDtypeStruct((m, n), BF16),
        compiler_params=_params(("parallel",)),
        name="mem_kv",
    )(mem2d, wkv)


MEM_ROWS = 512


def _memattn_body(h_ref, kv_ref, wq_ref, wo_ref, g_ref, b_ref, o_ref, ctx_ref):
    tm, d = h_ref.shape
    dh = d // MEM_HEADS
    groups = [slice(r0, r0 + MEM_ROWS) for r0 in range(0, tm, MEM_ROWS)]
    qs = [_dot(h_ref[rs, :].astype(BF16), wq_ref[...]).astype(BF16) for rs in groups]

    def attend(rs, q):
        def scores(hd):
            cs = slice(hd * dh, (hd + 1) * dh)
            return _dot_nt(q[:, cs], kv_ref[:, cs]) * (dh ** -0.5)

        s = scores(0)
        ctx_prev = None
        for hd in range(MEM_HEADS):
            s_next = scores(hd + 1) if hd + 1 < MEM_HEADS else None
            s = s - jnp.max(s, axis=-1, keepdims=True)
            p = jnp.exp(s)
            p = p / jnp.sum(p, axis=-1, keepdims=True)
            ctx = _dot(p.astype(BF16), kv_ref[:, d + hd * dh:d + (hd + 1) * dh])
            if ctx_prev is not None:
                ctx_ref[rs, (hd - 1) * dh:hd * dh] = ctx_prev.astype(BF16)
            ctx_prev = ctx
            s = s_next
        ctx_ref[rs, (MEM_HEADS - 1) * dh:] = ctx_prev.astype(BF16)
        return _dot(ctx_ref[rs, :], wo_ref[...])

    fs = [attend(rs, q) for rs, q in zip(groups, qs)]
    for rs, f in zip(groups, fs):
        o_ref[rs, :] = _ln(ALPHA * h_ref[rs, :] + f, g_ref[...], b_ref[...])


MLP_FC = 1024


def _mlp_body(h_ref, wu_ref, bu_ref, wd_ref, bd_ref, g_ref, b_ref, o_ref):
    h = h_ref[...]
    hb = h.astype(BF16)
    f = None
    for c0 in range(0, wu_ref.shape[1], MLP_FC):
        cs = slice(c0, c0 + MLP_FC)
        u = jnp.maximum(_dot(hb, wu_ref[:, cs]) + bu_ref[:, cs], 0.0)
        d = _dot((u * u).astype(BF16), wd_ref[cs, :])
        f = d if f is None else f + d
    o_ref[...] = _ln(ALPHA * h + (f + bd_ref[...]), g_ref[...], b_ref[...])


def _tail_body(og_ref, od_ref, os_ref, h_ref, kv_ref, wg_ref, wd_ref, ws_ref, g1_ref, b1_ref,
               wq_ref, wo_ref, g2_ref, b2_ref, wu_ref, bu_ref, wdn_ref, bdn_ref, g3_ref, b3_ref,
               o_ref, hs_ref, ctx_ref):
    _outproj_body(og_ref, od_ref, os_ref, h_ref, wg_ref, wd_ref, ws_ref, g1_ref, b1_ref, hs_ref)
    _memattn_body(hs_ref, kv_ref, wq_ref, wo_ref, g2_ref, b2_ref, hs_ref, ctx_ref)
    _mlp_body(hs_ref, wu_ref, bu_ref, wdn_ref, bdn_ref, g3_ref, b3_ref, o_ref)


def _tail(og, od, os_, h, kv, consts, layer, batch, seq, n_mem, tm=512):
    m, d = h.shape
    nt = seq // tm
    row = lambda a: pl.BlockSpec((tm, a.shape[1]), lambda bb, t: (bb * nt + t, 0))
    const = lambda a: _layer_block(a, layer, buffered=True)
    return pl.pallas_call(
        _tail_body,
        grid=(batch, nt),
        in_specs=[row(og), row(od), row(os_), row(h),
                  pl.BlockSpec((n_mem, 2 * d), lambda bb, t: (bb, 0))] + [const(a) for a in consts],
        out_specs=pl.BlockSpec((tm, d), lambda bb, t: (bb * nt + t, 0)),
        out_shape=jax.ShapeDtypeStruct((m, d), F32),
        scratch_shapes=[pltpu.VMEM((tm, d), F32), pltpu.VMEM((tm, d), BF16)],
        compiler_params=_params(("parallel", "parallel")),
        name="tail",
    )(og, od, os_, h, kv, *consts)


def _pad_heads(w, heads, width, padded):
    lead = w.shape[:-1]
    w = w.reshape(lead + (heads, width))
    w = jnp.pad(w, [(0, 0)] * len(lead) + [(0, 0), (0, padded - width)])
    return w.reshape(lead + (heads * padded,))


def _pad_last(w, n):
    return jnp.pad(w, [(0, 0)] * (w.ndim - 1) + [(0, n - w.shape[-1])])


def _split_w_in(w_in):
    sizes = (GLA_HEADS * GLA_DK, GLA_HEADS * GLA_DK, GLA_HEADS * GLA_DV, GLA_RANK, GLA_HEADS * GLA_DV,
             DSA_HEADS * DSA_DH, DSA_DH, DSA_LATENT, IDX_HEADS * IDX_DIM, IDX_DIM, IDX_HEADS,
             SB_HEADS * SB_DH, SB_HEADS * SB_DH, SB_HEADS * SB_DH)
    w_in = w_in.astype(BF16)
    parts = []
    o = 0
    for s in sizes:
        parts.append(w_in[..., o:o + s])
        o += s
    gq, gk, gv, glr, gg, dq, dk, dv, iq, ik, iw, sq, sk, sv = parts
    pieces = {
        "gla": jnp.concatenate([
            _pad_heads(gq, GLA_HEADS, GLA_DK, GLA_DKP), _pad_heads(gk, GLA_HEADS, GLA_DK, GLA_DKP),
            _pad_heads(gg, GLA_HEADS, GLA_DV, GLA_DVP)], axis=-1),
        "gv": _pad_heads(gv, GLA_HEADS, GLA_DV, GLA_DVP),
        "glr": _pad_last(glr, LANE),
        "iq": iq,
        "dkk": jnp.concatenate([ik, dk], axis=-1),
        "iw": _pad_last(iw, LANE),
        "dv": dv,
        "dq": _pad_last(dq, PAD_W),
        "sq": _pad_last(sq, PAD_W),
        "sk": _pad_last(sk, PAD_W),
        "sv": _pad_last(sv, PAD_W),
    }
    for name, width, _, _, _ in INPROJ_STREAMS:
        assert pieces[name].shape[-1] == width, name
    return jnp.concatenate([pieces[s[0]] for s in INPROJ_STREAMS], axis=-1)


def kernel(x, mem, ln_in_g, ln_in_b, w_in, gla_gate_w2, gla_gate_b, gla_norm_g, dsa_w_uv, w_out,
           ln_mix_g, ln_mix_b, w_mem_q, w_mem_kv, w_mem_o, ln_mem_g, ln_mem_b,
           w_up, b_up, w_down, b_down, ln_ffn_g, ln_ffn_b):
    batch, seq, d = x.shape
    n_mem = mem.shape[1]
    depth = w_in.shape[0]
    x2 = x.reshape(batch * seq, d)
    mem2 = mem.reshape(batch * n_mem, d)
    rows = lambda a: a.reshape(depth, 1, -1)
    gw, dw = GLA_HEADS * GLA_DV, DSA_HEADS * DSA_DH

    w_all = _split_w_in(w_in)
    w2p = jnp.pad(_pad_heads(gla_gate_w2, GLA_HEADS, GLA_DK, GLA_DKP),
                  ((0, 0), (0, LANE - GLA_RANK), (0, 0))).astype(BF16)
    gbp = rows(_pad_heads(gla_gate_b, GLA_HEADS, GLA_DK, GLA_DKP))
    ngp = rows(_pad_last(gla_norm_g, GLA_DVP))
    wuvt = jnp.swapaxes(dsa_w_uv, 2, 3).astype(BF16)
    wo = w_out.astype(BF16)
    wg = jnp.pad(wo[:, :gw].reshape(depth, GLA_HEADS, GLA_DV, d), ((0, 0), (0, 0), (0, GLA_DVP - GLA_DV), (0, 0)))
    wg = wg.reshape(depth, GLA_HEADS * GLA_DVP, d)
    wkv = w_mem_kv.astype(BF16)
    consts = (wg, wo[:, gw:gw + dw], wo[:, gw + dw:], rows(ln_mix_g), rows(ln_mix_b),
              w_mem_q.astype(BF16), w_mem_o.astype(BF16), rows(ln_mem_g), rows(ln_mem_b),
              w_up.astype(BF16), rows(b_up), w_down.astype(BF16), rows(b_down),
              rows(ln_ffn_g), rows(ln_ffn_b))

    h = None
    for l in range(depth):
        if l == 0:
            outs = _inproj(x2, w_all, l, ln=(ln_in_g, ln_in_b))
            h = outs[-1]
            outs = outs[:-1]
        else:
            outs = _inproj(h, w_all, l)
        p_gla, p_gv, p_gvt, p_glr, p_iq, p_dkk, p_iw, p_dvt, p_dq, p_sq, p_sk, p_svt = outs
        o_gla = _gla(p_gla, p_gv, p_gvt, p_glr, w2p, gbp, ngp, l, batch, seq)
        o_dsa = _dsa(p_iq, p_dq, p_iw, p_dkk, p_dvt, wuvt, l, batch, seq)
        o_sb = _sb(p_sq, p_sk, p_svt, batch, seq)
        kv = _memkv(mem2, wkv, l)
        h = _tail(o_gla, o_dsa, o_sb, h, kv, consts, l, batch, seq, n_mem)
    return h.reshape(batch, seq, d)
```
